```python
import jax, jax.numpy as jnp
from jax import lax
import numpy as np


D_MODEL = 1024
BATCH = 8
SEQ = 2048
DEPTH = 4
DEC_BATCH = 128
DEC_SEQ = 1
PAST_LEN = 16384
PAGE_SIZE = 128

HEAD_SIZE = 64
C_RWKV = D_MODEL // 2
N_RWKV_HEADS = C_RWKV // HEAD_SIZE
D_DECAY_LORA = 64
D_ICLR_LORA = 64
D_GATE_LORA = 128
GN_EPS = 64e-5
C_CONV = D_MODEL // 2
CONV_WIDTH = 31
N_MEM = 256
N_XHEADS = 4
XHEAD_DIM = D_MODEL // N_XHEADS
D_FF = -(-8 * D_MODEL // (3 * 256)) * 256
W_RWKV = 3 * C_RWKV + D_DECAY_LORA + D_ICLR_LORA + D_GATE_LORA
W_IN = W_RWKV + 2 * C_CONV + 2 * D_MODEL
NORM_EPS = 1e-6
LN_EPS = 1e-5

kernel_name = "rwkv7_conformer_gated_hybrid_step"


def rms_norm(x, g):
    xf = x.astype(jnp.float32)
    y = xf * lax.rsqrt(jnp.mean(xf * xf, axis=-1, keepdims=True) + NORM_EPS)
    return y.astype(x.dtype) * g


def layer_norm(x, g, b):
    xf = x.astype(jnp.float32)
    mu = jnp.mean(xf, axis=-1, keepdims=True)
    var = jnp.mean(jnp.square(xf - mu), axis=-1, keepdims=True)
    return ((xf - mu) * lax.rsqrt(var + LN_EPS)).astype(x.dtype) * g + b


def wkv_scan(S0, r, w, k, v, a, b):
    def step(S, inp):
        r_t, w_t, k_t, v_t, a_t, b_t = inp
        sa = jnp.einsum("bhij,bhj->bhi", S, a_t)
        S = S * w_t[:, :, None, :] + sa[..., None] * b_t[:, :, None, :] + v_t[..., None] * k_t[:, :, None, :]
        y = jnp.einsum("bhij,bhj->bhi", S, r_t)
        return S, y
    seq = tuple(jnp.moveaxis(t, 1, 0) for t in (r, w, k, v, a, b))
    S, ys = lax.scan(step, S0, seq)
    return jnp.moveaxis(ys, 0, 1), S


def rwkv_branch(p, shift0, S0, lp):
    B, T, _ = p.shape
    f32 = jnp.float32
    p_prev = jnp.concatenate([shift0[:, None, :].astype(p.dtype), p[:, :-1]], axis=1)
    xs = p + (p_prev - p) * lp["tshift_mu"]
    c = C_RWKV
    r, k, v, wd, ad, gd = jnp.split(xs, [c, 2 * c, 3 * c, 3 * c + D_DECAY_LORA,
                                          3 * c + D_DECAY_LORA + D_ICLR_LORA], axis=-1)
    w = -jax.nn.softplus(-(lp["decay_w0"] + jnp.tanh(wd) @ lp["decay_w2"]).astype(f32)) - 0.5
    decay = jnp.exp(-jnp.exp(w))
    a = jax.nn.sigmoid((lp["iclr_a0"] + ad @ lp["iclr_a2"]).astype(f32))
    g = jax.nn.sigmoid(gd) @ lp["gate_g2"]
    heads = lambda t: t.reshape(B, T, N_RWKV_HEADS, HEAD_SIZE)
    kk = heads((k * lp["k_k"]).astype(f32))
    kk = kk * lax.rsqrt(jnp.maximum(jnp.sum(kk * kk, axis=-1, keepdims=True), 1e-24))
    k_h = heads(k.astype(f32) * (1.0 + (a - 1.0) * lp["k_a"].astype(f32)))
    a_h = heads(a)
    r_h = heads(r.astype(f32))
    v_h = heads(v.astype(f32))
    y, S = wkv_scan(S0.astype(f32), r_h, heads(decay), k_h, v_h, -kk, kk * a_h)
    mu = jnp.mean(y, axis=-1, keepdims=True)
    var = jnp.mean(jnp.square(y - mu), axis=-1, keepdims=True)
    yn = ((y - mu) * lax.rsqrt(var + GN_EPS)).reshape(B, T, C_RWKV)
    yn = yn * lp["lnx_g"].astype(f32) + lp["lnx_b"].astype(f32)
    bonus = (jnp.sum(r_h * k_h * lp["r_k"].astype(f32), axis=-1, keepdims=True) * v_h).reshape(B, T, C_RWKV)
    out = ((yn + bonus).astype(p.dtype) * g) @ lp["w_out_rwkv"]
    return out, S.astype(p.dtype), p[:, -1]


def conformer_branch(u, buf0, lp):
    u = u + lp["glu_b"]
    val, gate = jnp.split(u, 2, axis=-1)
    glu = val * jax.nn.sigmoid(gate)
    xpad = jnp.concatenate([buf0.astype(glu.dtype), glu], axis=1)
    y = lax.conv_general_dilated(xpad, lp["conv_w"][:, None, :].astype(xpad.dtype),
                                 window_strides=(1,), padding="VALID",
                                 dimension_numbers=("NWC", "WIO", "NWC"),
                                 feature_group_count=C_CONV) + lp["conv_b"]
    y = jax.nn.silu(layer_norm(y, lp["conv_ln_g"], lp["conv_ln_b"]))
    return y @ lp["w_pw"] + lp["b_pw"], xpad[:, -(CONV_WIDTH - 1):]


def mem_kv(mem, g_mem, w_k, w_v):
    B, M, _ = mem.shape
    h = rms_norm(mem, g_mem)
    return ((h @ w_k).reshape(B, M, N_XHEADS, XHEAD_DIM),
            (h @ w_v).reshape(B, M, N_XHEADS, XHEAD_DIM))


def decoder_layer(x, mem_k, mem_v, S0, shift0, buf0, lp):
    B, T, _ = x.shape
    h = rms_norm(x, lp["g_mix_pre"])
    proj = h @ lp["w_in"]
    p_rwkv, p_conv, p_gate = jnp.split(proj, [W_RWKV, W_RWKV + 2 * C_CONV], axis=-1)
    y_a, S1, shift1 = rwkv_branch(p_rwkv, shift0, S0, lp)
    y_b, buf1 = conformer_branch(p_conv, buf0, lp)
    g_a, g_b = jnp.split(jax.nn.sigmoid(p_gate + lp["merge_gate_b"]), 2, axis=-1)
    mixed = (g_a * y_a + g_b * y_b) @ lp["w_out"]
    x = x + rms_norm(mixed, lp["g_mix_post"])
    h = rms_norm(x, lp["g_x_pre"])
    q = (h @ lp["w_q"]).reshape(B, T, N_XHEADS, XHEAD_DIM)
    s = jnp.einsum("bthd,bmhd->bhtm", q.astype(jnp.float32), mem_k.astype(jnp.float32)) * (XHEAD_DIM ** -0.5)
    pr = jax.nn.softmax(s, axis=-1)
    o = jnp.einsum("bhtm,bmhd->bthd", pr, mem_v.astype(jnp.float32)).astype(x.dtype).reshape(B, T, D_MODEL)
    x = x + rms_norm(o @ lp["w_o_x"], lp["g_x_post"])
    h = rms_norm(x, lp["g_ffn_pre"])
    gt, up = jnp.split(h @ lp["w_gate_up"], 2, axis=-1)
    f = (jax.nn.silu(gt) * up) @ lp["w_down"]
    x = x + rms_norm(f, lp["g_ffn_post"])
    return x, S1, shift1, buf1


def setup_inputs(seed: int = 0) -> dict:
    key = jax.random.key(seed)
    ks = iter(jax.random.split(key, 64))
    f32 = jnp.float32
    nrm = lambda shape, s: jax.random.normal(next(ks), shape, f32) * s
    gain = lambda shape: 1.0 + nrm(shape, 0.05)
    L = DEPTH
    d = {}
    d["x_prompt"] = nrm((BATCH, SEQ, D_MODEL), 1.0)
    d["x_sample"] = nrm((DEC_BATCH, DEC_SEQ, D_MODEL), 1.0)
    d["mem_prompt"] = nrm((BATCH, N_MEM, D_MODEL), 1.0)
    d["state_rwkv"] = nrm((L, DEC_BATCH, N_RWKV_HEADS, HEAD_SIZE, HEAD_SIZE), 0.3)
    d["state_shift"] = nrm((L, DEC_BATCH, W_RWKV), 1.0)
    d["state_conv"] = nrm((L, DEC_BATCH, CONV_WIDTH - 1, C_CONV), 0.5)
    d["cache_mem_k"] = nrm((L, DEC_BATCH, N_MEM, N_XHEADS, XHEAD_DIM), 1.0)
    d["cache_mem_v"] = nrm((L, DEC_BATCH, N_MEM, N_XHEADS, XHEAD_DIM), 1.0)
    d["w_in"] = nrm((L, D_MODEL, W_IN), D_MODEL ** -0.5)
    d["tshift_mu"] = jax.random.uniform(next(ks), (L, W_RWKV), f32)
    d["decay_w0"] = nrm((L, C_RWKV), 0.5) - 1.0
    d["decay_w2"] = nrm((L, D_DECAY_LORA, C_RWKV), 0.1)
    d["iclr_a0"] = nrm((L, C_RWKV), 0.5)
    d["iclr_a2"] = nrm((L, D_ICLR_LORA, C_RWKV), 0.1)
    d["gate_g2"] = nrm((L, D_GATE_LORA, C_RWKV), D_GATE_LORA ** -0.5)
    d["k_k"] = 0.85 + nrm((L, C_RWKV), 0.05)
    d["k_a"] = gain((L, C_RWKV))
    d["r_k"] = nrm((L, N_RWKV_HEADS, HEAD_SIZE), 0.1)
    d["lnx_g"] = gain((L, C_RWKV))
    d["lnx_b"] = nrm((L, C_RWKV), 0.01)
    d["w_out_rwkv"] = nrm((L, C_RWKV, D_MODEL), C_RWKV ** -0.5)
    d["glu_b"] = nrm((L, 2 * C_CONV), 0.01)
    d["conv_w"] = nrm((L, CONV_WIDTH, C_CONV), CONV_WIDTH ** -0.5)
    d["conv_b"] = nrm((L, C_CONV), 0.01)
    d["conv_ln_g"] = gain((L, C_CONV))
    d["conv_ln_b"] = nrm((L, C_CONV), 0.01)
    d["w_pw"] = nrm((L, C_CONV, D_MODEL), C_CONV ** -0.5)
    d["b_pw"] = nrm((L, D_MODEL), 0.01)
    d["merge_gate_b"] = nrm((L, 2 * D_MODEL), 0.01)
    d["w_out"] = nrm((L, D_MODEL, D_MODEL), D_MODEL ** -0.5)
    d["g_mix_pre"] = gain((L, D_MODEL))
    d["g_mix_post"] = gain((L, D_MODEL))
    d["g_mem"] = gain((L, D_MODEL))
    d["w_q"] = nrm((L, D_MODEL, D_MODEL), D_MODEL ** -0.5)
    d["w_k"] = nrm((L, D_MODEL, D_MODEL), D_MODEL ** -0.5)
    d["w_v"] = nrm((L, D_MODEL, D_MODEL), D_MODEL ** -0.5)
    d["w_o_x"] = nrm((L, D_MODEL, D_MODEL), D_MODEL ** -0.5)
    d["g_x_pre"] = gain((L, D_MODEL))
    d["g_x_post"] = gain((L, D_MODEL))
    d["g_ffn_pre"] = gain((L, D_MODEL))
    d["g_ffn_post"] = gain((L, D_MODEL))
    d["w_gate_up"] = nrm((L, D_MODEL, 2 * D_FF), D_MODEL ** -0.5)
    d["w_down"] = nrm((L, D_FF, D_MODEL), D_FF ** -0.5)
    return d


def reference(x_prompt, x_sample, mem_prompt, state_rwkv, state_shift, state_conv, cache_mem_k, cache_mem_v,
              w_in, tshift_mu, decay_w0, decay_w2, iclr_a0, iclr_a2, gate_g2, k_k, k_a, r_k, lnx_g, lnx_b,
              w_out_rwkv, glu_b, conv_w, conv_b, conv_ln_g, conv_ln_b, w_pw, b_pw, merge_gate_b, w_out,
              g_mix_pre, g_mix_post, g_mem, w_q, w_k, w_v, w_o_x, g_x_pre, g_x_post, g_ffn_pre, g_ffn_post,
              w_gate_up, w_down):
    xp, xs = x_prompt, x_sample
    Bp = x_prompt.shape[0]
    rwkv_p, shift_p, conv_p, memk_p, memv_p = [], [], [], [], []
    rwkv_s, shift_s, conv_s = [], [], []
    for l in range(DEPTH):
        lp = {"w_in": w_in[l], "tshift_mu": tshift_mu[l], "decay_w0": decay_w0[l], "decay_w2": decay_w2[l],
              "iclr_a0": iclr_a0[l], "iclr_a2": iclr_a2[l], "gate_g2": gate_g2[l], "k_k": k_k[l],
              "k_a": k_a[l], "r_k": r_k[l], "lnx_g": lnx_g[l], "lnx_b": lnx_b[l],
              "w_out_rwkv": w_out_rwkv[l], "glu_b": glu_b[l], "conv_w": conv_w[l], "conv_b": conv_b[l],
              "conv_ln_g": conv_ln_g[l], "conv_ln_b": conv_ln_b[l], "w_pw": w_pw[l], "b_pw": b_pw[l],
              "merge_gate_b": merge_gate_b[l], "w_out": w_out[l], "g_mix_pre": g_mix_pre[l],
              "g_mix_post": g_mix_post[l], "w_q": w_q[l], "w_o_x": w_o_x[l], "g_x_pre": g_x_pre[l],
              "g_x_post": g_x_post[l], "g_ffn_pre": g_ffn_pre[l], "g_ffn_post": g_ffn_post[l],
              "w_gate_up": w_gate_up[l], "w_down": w_down[l]}
        mk_p, mv_p = mem_kv(mem_prompt, g_mem[l], w_k[l], w_v[l])
        S0 = jnp.zeros((Bp, N_RWKV_HEADS, HEAD_SIZE, HEAD_SIZE), xp.dtype)
        sh0 = jnp.zeros((Bp, W_RWKV), xp.dtype)
        cb0 = jnp.zeros((Bp, CONV_WIDTH - 1, C_CONV), xp.dtype)
        xp, S_p, sh_p, cb_p = decoder_layer(xp, mk_p, mv_p, S0, sh0, cb0, lp)
        rwkv_p.append(S_p); shift_p.append(sh_p); conv_p.append(cb_p); memk_p.append(mk_p); memv_p.append(mv_p)
        xs, S_s, sh_s, cb_s = decoder_layer(xs, cache_mem_k[l], cache_mem_v[l], state_rwkv[l], state_shift[l],
                                            state_conv[l], lp)
        rwkv_s.append(S_s); shift_s.append(sh_s); conv_s.append(cb_s)
    return (xp, xs, jnp.stack(rwkv_p), jnp.stack(shift_p), jnp.stack(conv_p), jnp.stack(memk_p),
            jnp.stack(memv_p), jnp.stack(rwkv_s), jnp.stack(shift_s), jnp.stack(conv_s))
```

```python
import functools
import math

import jax
import jax.numpy as jnp
from jax import lax
from jax.experimental import pallas as pl
from jax.experimental.pallas import tpu as pltpu

F32 = jnp.float32
BF16 = jnp.bfloat16

HEAD_SIZE = 64
D_DECAY_LORA = 64
D_ICLR_LORA = 64
D_GATE_LORA = 128
CONV_WIDTH = 31
N_XHEADS = 4
GN_EPS = 64e-5
NORM_EPS = 1e-6
LN_EPS = 1e-5

LANES = 128
SUBLANES = 8
VMEM_LIMIT_BYTES = 56 * 1024 * 1024

WKV_CHUNK = 64


def _cparams(*sem):
    return pltpu.CompilerParams(dimension_semantics=sem, vmem_limit_bytes=VMEM_LIMIT_BYTES)


def _dot(a, b):
    return jnp.dot(a.astype(BF16), b.astype(BF16), preferred_element_type=F32)


def _dot_nt(a, b):
    return lax.dot_general(a.astype(BF16), b.astype(BF16), (((1,), (1,)), ((), ())),
                           preferred_element_type=F32)


def _split2(x):
    hi = x.astype(BF16)
    lo = (x - hi.astype(F32)).astype(BF16)
    return hi, lo


def _split3(x):
    hi = x.astype(BF16)
    r1 = x - hi.astype(F32)
    mid = r1.astype(BF16)
    lo = (r1 - mid.astype(F32)).astype(BF16)
    return hi, mid, lo


def _dot_x3(a, b):
    ah, al = _split2(a)
    bh, bl = _split2(b)
    d = functools.partial(jnp.dot, preferred_element_type=F32)
    return d(ah, bh) + (d(ah, bl) + d(al, bh))


def _dot_nt_x3(a, b):
    ah, al = _split2(a)
    bh, bl = _split2(b)
    d = functools.partial(lax.dot_general, dimension_numbers=(((1,), (1,)), ((), ())),
                          preferred_element_type=F32)
    return d(ah, bh) + (d(ah, bl) + d(al, bh))


def _segsum(x, bd):
    hi, lo = _split2(x)
    return (jnp.dot(hi, bd, preferred_element_type=F32) + jnp.dot(lo, bd, preferred_element_type=F32))


def _rms(x, g):
    return x * lax.rsqrt(jnp.mean(x * x, axis=-1, keepdims=True) + NORM_EPS) * g


def _sigmoid(x):
    return 1.0 / (1.0 + jnp.exp(-x))


def _row_tile(m, target):
    t = min(m, target)
    assert m % t == 0, (m, t)
    return t


def _full(shape):
    nd = len(shape)
    return pl.BlockSpec(shape, lambda *_: (0,) * nd)


def _norm_mm_kernel(x_ref, g_ref, w_ref, *o_refs, splits):
    h = _rms(x_ref[...], g_ref[...]).astype(BF16)
    off = 0
    for o_ref, n in zip(o_refs, splits):
        o_ref[...] = jnp.dot(h, w_ref[:, off:off + n], preferred_element_type=F32)
        off += n


def norm_mm(x, g, w, splits, tm):
    m, k = x.shape
    tm = _row_tile(m, tm)
    n_all = w.shape[1]
    assert sum(splits) == n_all and all(s % LANES == 0 for s in splits)
    return pl.pallas_call(
        functools.partial(_norm_mm_kernel, splits=tuple(splits)),
        grid=(m // tm,),
        in_specs=[pl.BlockSpec((tm, k), lambda i: (i, 0)), _full((1, k)), _full((k, n_all))],
        out_specs=[pl.BlockSpec((tm, n), lambda i: (i, 0)) for n in splits],
        out_shape=[jax.ShapeDtypeStruct((m, n), F32) for n in splits],
        compiler_params=_cparams("parallel"),
        name="norm_mm",
    )(x, g.reshape(1, k), w)


def _rwkv_prep_kernel(p_ref, pp_ref, mu_ref, w0_ref, w2_ref, a0_ref, a2_ref, g2_ref, kk_ref, ka_ref,
                      bd_ref, r_ref, lw_ref, k_ref, v_ref, a_ref, b_ref, g_ref, *, c):
    p = p_ref[...]
    xs = p + (pp_ref[...] - p) * mu_ref[...]
    r = xs[:, 0:c]
    k = xs[:, c:2 * c]
    v = xs[:, 2 * c:3 * c]
    wa = xs[:, 3 * c:3 * c + D_DECAY_LORA + D_ICLR_LORA]
    gd = xs[:, 3 * c + D_DECAY_LORA + D_ICLR_LORA:]
    z = w0_ref[...] + _dot(jnp.tanh(wa), w2_ref[...])
    lw_ref[...] = -math.exp(-0.5) * _sigmoid(z)
    a = _sigmoid(a0_ref[...] + _dot(wa, a2_ref[...]))
    g_ref[...] = _dot(_sigmoid(gd), g2_ref[...])
    kk = k * kk_ref[...]
    ss = _segsum(kk * kk, bd_ref[...])
    kk = kk * lax.rsqrt(jnp.maximum(ss, 1e-24))
    r_ref[...] = r
    k_ref[...] = k * (1.0 + (a - 1.0) * ka_ref[...])
    v_ref[...] = v
    a_ref[...] = -kk
    b_ref[...] = kk * a


def rwkv_prep(p, p_prev, lpk, tm):
    m, wr = p.shape
    c = lpk["c_rwkv"]
    tm = _row_tile(m, tm)
    row = lambda n: pl.BlockSpec((tm, n), lambda i: (i, 0))
    names = ("tshift_mu", "decay_w0", "decay_w2p", "iclr_a0", "iclr_a2p", "gate_g2", "k_k", "k_a", "bd")
    params = [lpk[n] for n in names]
    return pl.pallas_call(
        functools.partial(_rwkv_prep_kernel, c=c),
        grid=(m // tm,),
        in_specs=[row(wr), row(wr)] + [_full(a.shape) for a in params],
        out_specs=[row(c)] * 7,
        out_shape=[jax.ShapeDtypeStruct((m, c), F32)] * 7,
        compiler_params=_cparams("parallel"),
        name="rwkv_prep",
    )(p, p_prev, *params)


def _wkv_chunk_kernel(r_ref, lw_ref, k_ref, v_ref, a_ref, b_ref, y_ref, s_ref, g_scr, *, L, npairs):
    ci = pl.program_id(1)

    @pl.when(ci == 0)
    def _():
        g_scr[...] = jnp.zeros_like(g_scr)

    lw = lw_ref[0]
    row = lax.broadcasted_iota(jnp.int32, (L, L), 0)
    col = lax.broadcasted_iota(jnp.int32, (L, L), 1)
    ltri = jnp.where(col <= row, 1.0, 0.0).astype(BF16)
    hi, mid, lo = _split3(lw)
    d = functools.partial(jnp.dot, preferred_element_type=F32)
    cs = d(ltri, hi) + (d(ltri, mid) + d(ltri, lo))
    pin = jnp.exp(cs)
    pex = jnp.exp(cs - lw)
    pinv = jnp.exp(-cs)
    at = a_ref[0] * pex
    rt = r_ref[0] * pin
    bt = b_ref[0] * pinv
    kt = k_ref[0] * pinv
    v = v_ref[0]
    p_last = pin[L - 1:L, :]

    m0 = lax.broadcasted_iota(jnp.int32, (L, LANES), 1) < HEAD_SIZE
    i2 = lax.broadcasted_iota(jnp.int32, (2 * L, 2 * L), 0)
    j2 = lax.broadcasted_iota(jnp.int32, (2 * L, 2 * L), 1)
    same = (i2 < L) == (j2 < L)
    tril_s = same & (j2 < i2)
    tril_i = same & (j2 <= i2)
    eye = jnp.where(i2 == j2, 1.0, 0.0)

    ys = []
    for p in range(npairs):
        sl = slice(p * LANES, (p + 1) * LANES)

        def stack(x):
            xp = x[:, sl]
            return jnp.concatenate([jnp.where(m0, xp, 0.0), jnp.where(m0, 0.0, xp)], axis=0)

        vm = stack(v)
        lhs = jnp.concatenate([stack(at), stack(rt)], axis=0)
        rhs = jnp.concatenate([stack(bt), stack(kt)], axis=0)
        gm = _dot_nt_x3(lhs, rhs)
        a_ab = jnp.where(tril_s, gm[:2 * L, :2 * L], 0.0)
        a_ak = jnp.where(tril_s, gm[:2 * L, 2 * L:], 0.0)
        a_rb = jnp.where(tril_i, gm[2 * L:, :2 * L], 0.0)
        a_rk = jnp.where(tril_i, gm[2 * L:, 2 * L:], 0.0)
        mv = _dot_x3(jnp.concatenate([a_ak, a_rk], axis=0), vm)
        g = g_scr[p]
        gh = _dot_nt_x3(lhs, g)
        x0 = gh[:2 * L] + mv[:2 * L]
        t = eye + a_ab
        ak = a_ab
        for _ in range(int(math.log2(L)) - 1):
            ak = _dot_x3(ak, ak)
            t = t + _dot_x3(ak, t)
        u = _dot_x3(t, x0)
        yy = gh[2 * L:] + mv[2 * L:] + _dot_x3(a_rb, u)
        ys.append(yy[:L] + yy[L:])
        upd = _dot_x3(jnp.concatenate([u, vm], axis=0).T, rhs)
        g_scr[p] = (g + upd) * p_last[:, sl]

    y_ref[0] = jnp.concatenate(ys, axis=1)

    @pl.when(ci == pl.num_programs(1) - 1)
    def _():
        s_ref[0] = g_scr[...]


def wkv_chunked(r, lw, k, v, a, b):
    bsz, t, c = r.shape
    L = WKV_CHUNK
    assert t % L == 0 and c % LANES == 0
    npairs = c // LANES
    seq = pl.BlockSpec((1, L, c), lambda bi, ci: (bi, ci, 0))
    y, s = pl.pallas_call(
        functools.partial(_wkv_chunk_kernel, L=L, npairs=npairs),
        grid=(bsz, t // L),
        in_specs=[seq] * 6,
        out_specs=[seq, pl.BlockSpec((1, npairs, LANES, LANES), lambda bi, ci: (bi, 0, 0, 0))],
        out_shape=[jax.ShapeDtypeStruct((bsz, t, c), F32),
                   jax.ShapeDtypeStruct((bsz, npairs, LANES, LANES), F32)],
        scratch_shapes=[pltpu.VMEM((npairs, LANES, LANES), F32)],
        compiler_params=_cparams("parallel", "arbitrary"),
        name="wkv_chunked",
    )(r, lw, k, v, a, b)
    n = HEAD_SIZE
    s = jnp.stack([s[:, :, :n, :n], s[:, :, n:, n:]], axis=2)
    return y, s.reshape(bsz, 2 * npairs, n, n)


def _wkv_step_kernel(s_ref, r_ref, lw_ref, k_ref, v_ref, a_ref, b_ref, so_ref, y_ref, *, bb, nh):
    n = HEAD_SIZE
    eye = jnp.where(lax.broadcasted_iota(jnp.int32, (n, n), 0) == lax.broadcasted_iota(jnp.int32, (n, n), 1),
                    1.0, 0.0)

    def body(i, carry):
        r, k, v, a, b = r_ref[i], k_ref[i], v_ref[i], a_ref[i], b_ref[i]
        w = jnp.exp(lw_ref[i])
        rows = []
        for h in range(nh):
            hs = slice(h, h + 1)
            s = s_ref[i, h]
            sa = jnp.sum(s * a[hs], axis=1, keepdims=True)
            v_col = jnp.sum(eye * v[hs], axis=1, keepdims=True)
            s = s * w[hs] + sa * b[hs] + v_col * k[hs]
            so_ref[i, h] = s
            y_col = jnp.sum(s * r[hs], axis=1, keepdims=True)
            rows.append(jnp.sum(eye * y_col, axis=0, keepdims=True))
        y_ref[i] = jnp.concatenate(rows, axis=0)
        return carry

    lax.fori_loop(0, bb, body, 0)


def wkv_step(s0, r, lw, k, v, a, b):
    bsz, nh, n, _ = s0.shape
    bb = _row_tile(bsz, 8)
    vec = pl.BlockSpec((bb, nh, n), lambda i: (i, 0, 0))
    st = pl.BlockSpec((bb, nh, n, n), lambda i: (i, 0, 0, 0))
    hv = lambda x: x.reshape(bsz, nh, n)
    s1, y = pl.pallas_call(
        functools.partial(_wkv_step_kernel, bb=bb, nh=nh),
        grid=(bsz // bb,),
        in_specs=[st] + [vec] * 6,
        out_specs=[st, vec],
        out_shape=[jax.ShapeDtypeStruct(s0.shape, F32), jax.ShapeDtypeStruct((bsz, nh, n), F32)],
        compiler_params=_cparams("parallel"),
        name="wkv_step",
    )(s0, hv(r), hv(lw), hv(k), hv(v), hv(a), hv(b))
    return y.reshape(bsz, nh * n), s1


CONV_PAD = 32


def _glu(u, gb_ref, c):
    u = u + gb_ref[...]
    return u[:, :c] * _sigmoid(u[:, c:])


def _ln_swish(y, g_ref, b_ref):
    mu = jnp.mean(y, axis=-1, keepdims=True)
    dlt = y - mu
    var = jnp.mean(dlt * dlt, axis=-1, keepdims=True)
    y = dlt * lax.rsqrt(var + LN_EPS) * g_ref[...] + b_ref[...]
    return y * _sigmoid(y)


def _conv_seq_kernel(u_ref, buf_ref, gb_ref, cw_ref, cb_ref, lg_ref, lb_ref, o_ref, nb_ref, xp_scr, *, tt, c):
    ti = pl.program_id(1)

    @pl.when(ti == 0)
    def _():
        xp_scr[0:CONV_PAD, :] = buf_ref[0]

    xp_scr[CONV_PAD:CONV_PAD + tt, :] = _glu(u_ref[0], gb_ref, c)
    base = CONV_PAD - (CONV_WIDTH - 1)
    acc = jnp.zeros((tt, c), F32) + cb_ref[...]
    for w in range(CONV_WIDTH):
        acc = acc + xp_scr[base + w:base + w + tt, :] * cw_ref[w:w + 1, :]
    o_ref[0] = _ln_swish(acc, lg_ref, lb_ref)
    tail = xp_scr[tt:tt + CONV_PAD, :]
    xp_scr[0:CONV_PAD, :] = tail

    @pl.when(ti == pl.num_programs(1) - 1)
    def _():
        nb_ref[0] = tail


def conv_seq(u, buf0, lpk, tt):
    bsz, t, c2 = u.shape
    c = c2 // 2
    tt = _row_tile(t, tt)
    pad = CONV_PAD - (CONV_WIDTH - 1)
    bufp = jnp.pad(buf0, ((0, 0), (pad, 0), (0, 0)))
    params = [lpk[n] for n in ("glu_b", "conv_w", "conv_b", "conv_ln_g", "conv_ln_b")]
    hist = pl.BlockSpec((1, CONV_PAD, c), lambda bi, ti: (bi, 0, 0))
    out, nb = pl.pallas_call(
        functools.partial(_conv_seq_kernel, tt=tt, c=c),
        grid=(bsz, t // tt),
        in_specs=[pl.BlockSpec((1, tt, c2), lambda bi, ti: (bi, ti, 0)), hist] + [_full(a.shape) for a in params],
        out_specs=[pl.BlockSpec((1, tt, c), lambda bi, ti: (bi, ti, 0)), hist],
        out_shape=[jax.ShapeDtypeStruct((bsz, t, c), F32), jax.ShapeDtypeStruct((bsz, CONV_PAD, c), F32)],
        scratch_shapes=[pltpu.VMEM((CONV_PAD + tt, c), F32)],
        compiler_params=_cparams("parallel", "arbitrary"),
        name="conv_seq",
    )(u, bufp, *params)
    return out, nb[:, pad:]


def _conv_step_kernel(u_ref, buf_ref, gb_ref, cw_ref, cb_ref, lg_ref, lb_ref, o_ref, nb_ref, *, bb, c):
    hist = CONV_WIDTH - 1
    glu = _glu(u_ref[...], gb_ref, c)
    w_hist = cw_ref[0:hist, :]
    w_last = cw_ref[hist:hist + 1, :]
    rows = []
    for i in range(bb):
        rows.append(jnp.sum(buf_ref[i] * w_hist, axis=0, keepdims=True))
    acc = jnp.concatenate(rows, axis=0) + glu * w_last + cb_ref[...]
    o_ref[...] = _ln_swish(acc, lg_ref, lb_ref)
    for i in range(bb):
        nb_ref[i, 0:hist - 1, :] = buf_ref[i, 1:hist, :]
        nb_ref[i, hist - 1:hist, :] = glu[i:i + 1, :]


def conv_step(u, buf0, lpk):
    bsz, c2 = u.shape
    c = c2 // 2
    bb = _row_tile(bsz, 8)
    params = [lpk[n] for n in ("glu_b", "conv_w", "conv_b", "conv_ln_g", "conv_ln_b")]
    hist = pl.BlockSpec((bb, CONV_WIDTH - 1, c), lambda i: (i, 0, 0))
    return pl.pallas_call(
        functools.partial(_conv_step_kernel, bb=bb, c=c),
        grid=(bsz // bb,),
        in_specs=[pl.BlockSpec((bb, c2), lambda i: (i, 0)), hist] + [_full(a.shape) for a in params],
        out_specs=[pl.BlockSpec((bb, c), lambda i: (i, 0)), hist],
        out_shape=[jax.ShapeDtypeStruct((bsz, c), F32), jax.ShapeDtypeStruct(buf0.shape, F32)],
        compiler_params=_cparams("parallel"),
        name="conv_step",
    )(u, buf0, *params)


def _merge_kernel(x_ref, y_ref, r_ref, k_ref, v_ref, g_ref, cv_ref, pg_ref, bd_ref, lg_ref, lb_ref, rk_ref,
                  wa_ref, wp_ref, bp_ref, mb_ref, wo_ref, gp_ref, o_ref, *, d):
    bd = bd_ref[...]
    inv_n = 1.0 / HEAD_SIZE
    y = y_ref[...]
    mu = _segsum(y, bd) * inv_n
    dlt = y - mu
    var = _segsum(dlt * dlt, bd) * inv_n
    yn = dlt * lax.rsqrt(var + GN_EPS) * lg_ref[...] + lb_ref[...]
    v = v_ref[...]
    bonus = _segsum(r_ref[...] * k_ref[...] * rk_ref[...], bd) * v
    za = (yn + bonus) * g_ref[...]
    ya = _dot(za, wa_ref[...])
    yb = _dot(cv_ref[...], wp_ref[...]) + bp_ref[...]
    gate = _sigmoid(pg_ref[...] + mb_ref[...])
    mixed = _dot(gate[:, :d] * ya + gate[:, d:] * yb, wo_ref[...])
    o_ref[...] = x_ref[...] + _rms(mixed, gp_ref[...])


def merge(x, y, r, k, v, g, cv, pg, lpk, tm):
    m, d = x.shape
    c = y.shape[1]
    tm = _row_tile(m, tm)
    row = lambda n: pl.BlockSpec((tm, n), lambda i: (i, 0))
    names = ("bd", "lnx_g", "lnx_b", "r_k", "w_out_rwkv", "w_pw", "b_pw", "merge_gate_b", "w_out", "g_mix_post")
    params = [lpk[n] for n in names]
    return pl.pallas_call(
        functools.partial(_merge_kernel, d=d),
        grid=(m // tm,),
        in_specs=[row(d)] + [row(c)] * 6 + [row(2 * d)] + [_full(a.shape) for a in params],
        out_specs=row(d),
        out_shape=jax.ShapeDtypeStruct((m, d), F32),
        compiler_params=_cparams("parallel"),
        name="merge",
    )(x, y, r, k, v, g, cv, pg, *params)


def _xattn_kernel(x_ref, mk_ref, mv_ref, gq_ref, wq_ref, wo_ref, gp_ref, o_ref, *, dh):
    x = x_ref[0]
    q = _dot(_rms(x, gq_ref[...]), wq_ref[...])
    mk = mk_ref[0].astype(BF16)
    mv = mv_ref[0].astype(BF16)
    scale = dh ** -0.5
    outs = []
    for h in range(N_XHEADS):
        sl = slice(h * dh, (h + 1) * dh)
        s = _dot_nt(q[:, sl], mk[:, sl]) * scale
        s = s - jnp.max(s, axis=-1, keepdims=True)
        e = jnp.exp(s)
        pr = e / jnp.sum(e, axis=-1, keepdims=True)
        outs.append(_dot(pr, mv[:, sl]))
    o = jnp.concatenate(outs, axis=1)
    o_ref[0] = x + _rms(_dot(o, wo_ref[...]), gp_ref[...])


def xattn(x, mk, mv, lpk, tq):
    bsz, t, d = x.shape
    nm = mk.shape[1]
    tq = _row_tile(t, tq)
    params = [lpk[n] for n in ("g_x_pre", "w_q", "w_o_x", "g_x_post")]
    xs = pl.BlockSpec((1, tq, d), lambda bi, ti: (bi, ti, 0))
    ms = pl.BlockSpec((1, nm, d), lambda bi, ti: (bi, 0, 0))
    return pl.pallas_call(
        functools.partial(_xattn_kernel, dh=d // N_XHEADS),
        grid=(bsz, t // tq),
        in_specs=[xs, ms, ms] + [_full(a.shape) for a in params],
        out_specs=xs,
        out_shape=jax.ShapeDtypeStruct(x.shape, F32),
        compiler_params=_cparams("parallel", "arbitrary"),
        name="xattn",
    )(x, mk, mv, *params)


def _ffn_kernel(x_ref, gpre_ref, wg_ref, wu_ref, wd_ref, gpost_ref, o_ref, h_scr, acc_scr):
    f = pl.program_id(1)

    @pl.when(f == 0)
    def _():
        h_scr[...] = _rms(x_ref[...], gpre_ref[...]).astype(BF16)
        acc_scr[...] = jnp.zeros_like(acc_scr)

    h = h_scr[...]
    gt = jnp.dot(h, wg_ref[...], preferred_element_type=F32)
    up = jnp.dot(h, wu_ref[...], preferred_element_type=F32)
    acc_scr[...] += _dot(gt * _sigmoid(gt) * up, wd_ref[...])

    @pl.when(f == pl.num_programs(1) - 1)
    def _():
        o_ref[...] = x_ref[...] + _rms(acc_scr[...], gpost_ref[...])


def ffn(x, lpk, tm):
    m, d = x.shape
    w_gu, w_dn = lpk["w_gate_up"], lpk["w_down"]
    dff = w_dn.shape[0]
    nf = 2
    fc = dff // nf
    assert fc * nf == dff and fc % LANES == 0
    tm = _row_tile(m, tm)
    return pl.pallas_call(
        _ffn_kernel,
        grid=(m // tm, nf),
        in_specs=[pl.BlockSpec((tm, d), lambda i, f: (i, 0)), _full((1, d)),
                  pl.BlockSpec((d, fc), lambda i, f: (0, f)),
                  pl.BlockSpec((d, fc), lambda i, f: (0, f + nf)),
                  pl.BlockSpec((fc, d), lambda i, f: (f, 0)), _full((1, d))],
        out_specs=pl.BlockSpec((tm, d), lambda i, f: (i, 0)),
        out_shape=jax.ShapeDtypeStruct((m, d), F32),
        scratch_shapes=[pltpu.VMEM((tm, d), BF16), pltpu.VMEM((tm, d), F32)],
        compiler_params=_cparams("parallel", "arbitrary"),
        name="ffn",
    )(x, lpk["g_ffn_pre"], w_gu, w_gu, w_dn, lpk["g_ffn_post"])


def _layer_params(l, w):
    row = lambda a: a[l].reshape(1, -1)
    c = w["w_out_rwkv"].shape[1]
    lora_rows = D_DECAY_LORA + D_ICLR_LORA
    w2p = jnp.zeros((lora_rows, c), F32).at[:D_DECAY_LORA].set(w["decay_w2"][l])
    a2p = jnp.zeros((lora_rows, c), F32).at[D_DECAY_LORA:].set(w["iclr_a2"][l])
    head = jnp.arange(c) // HEAD_SIZE
    lpk = {
        "c_rwkv": c,
        "bd": (head[:, None] == head[None, :]).astype(BF16),
        "decay_w2p": w2p.astype(BF16), "iclr_a2p": a2p.astype(BF16),
        "conv_w": jnp.pad(w["conv_w"][l], ((0, CONV_PAD - CONV_WIDTH), (0, 0))),
    }
    for n in ("w_in", "gate_g2", "w_out_rwkv", "w_pw", "w_out", "w_q", "w_o_x", "w_gate_up", "w_down"):
        lpk[n] = w[n][l].astype(BF16)
    lpk["w_kv"] = jnp.concatenate([w["w_k"][l], w["w_v"][l]], axis=1).astype(BF16)
    for n in ("tshift_mu", "decay_w0", "iclr_a0", "k_k", "k_a", "r_k", "lnx_g", "lnx_b", "glu_b", "conv_b",
              "conv_ln_g", "conv_ln_b", "b_pw", "merge_gate_b", "g_mix_pre", "g_mix_post", "g_mem", "g_x_pre",
              "g_x_post", "g_ffn_pre", "g_ffn_post"):
        lpk[n] = row(w[n])
    return lpk


def _decoder_layer(x, mem_k, mem_v, s0, shift0, buf0, lpk, tm):
    bsz, t, d = x.shape
    m = bsz * t
    c = lpk["c_rwkv"]
    wr = 3 * c + D_DECAY_LORA + D_ICLR_LORA + D_GATE_LORA
    x2 = x.reshape(m, d)
    p_rwkv, p_conv, p_gate = norm_mm(x2, lpk["g_mix_pre"], lpk["w_in"], (wr, 2 * c, 2 * d), tm)
    p3 = p_rwkv.reshape(bsz, t, wr)
    p_prev = jnp.concatenate([shift0[:, None, :], p3[:, :-1]], axis=1).reshape(m, wr)
    r, lw, k, v, a, b, g = rwkv_prep(p_rwkv, p_prev, lpk, tm)
    if s0 is None:
        seq = lambda z: z.reshape(bsz, t, c)
        y, s1 = wkv_chunked(seq(r), seq(lw), seq(k), seq(v), seq(a), seq(b))
        y = y.reshape(m, c)
        cv, buf1 = conv_seq(p_conv.reshape(bsz, t, 2 * c), buf0, lpk, 256)
        cv = cv.reshape(m, c)
    else:
        y, s1 = wkv_step(s0, r, lw, k, v, a, b)
        cv, buf1 = conv_step(p_conv, buf0, lpk)
    x2 = merge(x2, y, r, k, v, g, cv, p_gate, lpk, tm)
    if t == 1:
        xq = jnp.broadcast_to(x2[:, None, :], (bsz, SUBLANES, d))
        x2 = xattn(xq, mem_k, mem_v, lpk, SUBLANES)[:, 0]
    else:
        x2 = xattn(x2.reshape(bsz, t, d), mem_k, mem_v, lpk, 256).reshape(m, d)
    x2 = ffn(x2, lpk, 2 * tm)
    return x2.reshape(bsz, t, d), s1, p3[:, -1], buf1


def kernel(x_prompt, x_sample, mem_prompt, state_rwkv, state_shift, state_conv, cache_mem_k, cache_mem_v, w_in, tshift_mu, decay_w0, decay_w2, iclr_a0, iclr_a2, gate_g2, k_k, k_a, r_k, lnx_g, lnx_b, w_out_rwkv, glu_b, conv_w, conv_b, conv_ln_g, conv_ln_b, w_pw, b_pw, merge_gate_b, w_out, g_mix_pre, g_mix_post, g_mem, w_q, w_k, w_v, w_o_x, g_x_pre, g_x_post, g_ffn_pre, g_ffn_post, w_gate_up, w_down):
    w = dict(w_in=w_in, tshift_mu=tshift_mu, decay_w0=decay_w0, decay_w2=decay_w2, iclr_a0=iclr_a0,
             iclr_a2=iclr_a2, gate_g2=gate_g2, k_k=k_k, k_a=k_a, r_k=r_k, lnx_g=lnx_g, lnx_b=lnx_b,
             w_out_rwkv=w_out_rwkv, glu_b=glu_b, conv_w=conv_w, conv_b=conv_b, conv_ln_g=conv_ln_g,
             conv_ln_b=conv_ln_b, w_pw=w_pw, b_pw=b_pw, merge_gate_b=merge_gate_b, w_out=w_out,
             g_mix_pre=g_mix_pre, g_mix_post=g_mix_post, g_mem=g_mem, w_q=w_q, w_k=w_k, w_v=w_v, w_o_x=w_o_x,
             g_x_pre=g_x_pre, g_x_post=g_x_post, g_ffn_pre=g_ffn_pre, g_ffn_post=g_ffn_post,
             w_gate_up=w_gate_up, w_down=w_down)
    depth = w_in.shape[0]
    bp, _, d = x_prompt.shape
    bs = x_sample.shape[0]
    nm = mem_prompt.shape[1]
    c = w_out_rwkv.shape[1]
    wr = state_shift.shape[-1]
    xp, xs = x_prompt, x_sample
    mem2 = mem_prompt.reshape(bp * nm, d)
    outs = {n: [] for n in ("rwkv_p", "shift_p", "conv_p", "memk_p", "memv_p", "rwkv_s", "shift_s", "conv_s")}
    for l in range(depth):
        lpk = _layer_params(l, w)
        mk, mv = norm_mm(mem2, lpk["g_mem"], lpk["w_kv"], (d, d), 256)
        mk, mv = mk.reshape(bp, nm, d), mv.reshape(bp, nm, d)
        xp, s_p, sh_p, cb_p = _decoder_layer(xp, mk, mv, None, jnp.zeros((bp, wr), F32),
                                             jnp.zeros((bp, CONV_WIDTH - 1, c), F32), lpk, 256)
        xs, s_s, sh_s, cb_s = _decoder_layer(xs, cache_mem_k[l].reshape(bs, nm, d), cache_mem_v[l].reshape(bs, nm, d),
                                             state_rwkv[l], state_shift[l], state_conv[l], lpk, 128)
        dh = d // N_XHEADS
        outs["rwkv_p"].append(s_p); outs["shift_p"].append(sh_p); outs["conv_p"].append(cb_p)
        outs["memk_p"].append(mk.reshape(bp, nm, N_XHEADS, dh)); outs["memv_p"].append(mv.reshape(bp, nm, N_XHEADS, dh))
        outs["rwkv_s"].append(s_s); outs["shift_s"].append(sh_s); outs["conv_s"].append(cb_s)
    st = {n: jnp.stack(v) for n, v in outs.items()}
    return (xp, xs, st["rwkv_p"], st["shift_p"], st["conv_p"], st["memk_p"], st["memv_p"],
            st["rwkv_s"], st["shift_s"], st["conv_s"])
```

```python
import functools
import math

import jax
import jax.numpy as jnp
from jax import lax
from jax.experimental import pallas as pl
from jax.experimental.pallas import tpu as pltpu

F32 = jnp.float32
BF16 = jnp.bfloat16

HEAD_SIZE = 64
D_DECAY_LORA = 64
D_ICLR_LORA = 64
D_GATE_LORA = 128
CONV_WIDTH = 31
N_XHEADS = 4
GN_EPS = 64e-5
NORM_EPS = 1e-6
LN_EPS = 1e-5

LANES = 128
SUBLANES = 8
VMEM_LIMIT_BYTES = 56 * 1024 * 1024

WKV_CHUNK = 64


def _cparams(*sem):
    return pltpu.CompilerParams(dimension_semantics=sem, vmem_limit_bytes=VMEM_LIMIT_BYTES)


def _dot(a, b):
    return jnp.dot(a.astype(BF16), b.astype(BF16), preferred_element_type=F32)


def _dot_nt(a, b):
    return lax.dot_general(a.astype(BF16), b.astype(BF16), (((1,), (1,)), ((), ())),
                           preferred_element_type=F32)


def _split2(x):
    hi = x.astype(BF16)
    lo = (x - hi.astype(F32)).astype(BF16)
    return hi, lo


def _split3(x):
    hi = x.astype(BF16)
    r1 = x - hi.astype(F32)
    mid = r1.astype(BF16)
    lo = (r1 - mid.astype(F32)).astype(BF16)
    return hi, mid, lo


def _dot_x3(a, b):
    ah, al = _split2(a)
    bh, bl = _split2(b)
    d = functools.partial(jnp.dot, preferred_element_type=F32)
    return d(ah, bh) + (d(ah, bl) + d(al, bh))


def _dot_nt_x3(a, b):
    ah, al = _split2(a)
    bh, bl = _split2(b)
    d = functools.partial(lax.dot_general, dimension_numbers=(((1,), (1,)), ((), ())),
                          preferred_element_type=F32)
    return d(ah, bh) + (d(ah, bl) + d(al, bh))


def _segsum(x, bd):
    hi, lo = _split2(x)
    return (jnp.dot(hi, bd, preferred_element_type=F32) + jnp.dot(lo, bd, preferred_element_type=F32))


def _rms(x, g):
    return x * lax.rsqrt(jnp.mean(x * x, axis=-1, keepdims=True) + NORM_EPS) * g


def _sigmoid(x):
    return 1.0 / (1.0 + jnp.exp(-x))


def _row_tile(m, target):
    t = min(m, target)
    assert m % t == 0, (m, t)
    return t


def _full(shape):
    nd = len(shape)
    return pl.BlockSpec(shape, lambda *_: (0,) * nd)


def _norm_mm_kernel(x_ref, g_ref, w_ref, *o_refs, splits):
    h = _rms(x_ref[...], g_ref[...]).astype(BF16)
    off = 0
    for o_ref, n in zip(o_refs, splits):
        o_ref[...] = jnp.dot(h, w_ref[:, off:off + n], preferred_element_type=F32)
        off += n


def norm_mm(x, g, w, splits, tm):
    m, k = x.shape
    tm = _row_tile(m, tm)
    n_all = w.shape[1]
    assert sum(splits) == n_all and all(s % LANES == 0 for s in splits)
    return pl.pallas_call(
        functools.partial(_norm_mm_kernel, splits=tuple(splits)),
        grid=(m // tm,),
        in_specs=[pl.BlockSpec((tm, k), lambda i: (i, 0)), _full((1, k)), _full((k, n_all))],
        out_specs=[pl.BlockSpec((tm, n), lambda i: (i, 0)) for n in splits],
        out_shape=[jax.ShapeDtypeStruct((m, n), F32) for n in splits],
        compiler_params=_cparams("parallel"),
        name="norm_mm",
    )(x, g.reshape(1, k), w)


def _rwkv_prep_kernel(p_ref, halo_ref, sh_ref, mu_ref, w0_ref, w2_ref, a0_ref, a2_ref, g2_ref, kk_ref, ka_ref,
                      bd_ref, r_ref, lw_ref, k_ref, v_ref, a_ref, b_ref, g_ref, *, c, tiles_per_seq):
    p = p_ref[...]
    if tiles_per_seq == 0:
        pp = sh_ref[...]
    else:
        first = (pl.program_id(0) % tiles_per_seq) == 0
        prev_row = jnp.where(first, sh_ref[0], halo_ref[SUBLANES - 1:SUBLANES, :])
        rows = lax.broadcasted_iota(jnp.int32, p.shape, 0)
        pp = jnp.where(rows == 0, prev_row, pltpu.roll(p, 1, axis=0))
    xs = p + (pp - p) * mu_ref[...]
    r = xs[:, 0:c]
    k = xs[:, c:2 * c]
    v = xs[:, 2 * c:3 * c]
    wa = xs[:, 3 * c:3 * c + D_DECAY_LORA + D_ICLR_LORA]
    gd = xs[:, 3 * c + D_DECAY_LORA + D_ICLR_LORA:]
    z = w0_ref[...] + _dot(jnp.tanh(wa), w2_ref[...])
    lw_ref[...] = -math.exp(-0.5) * _sigmoid(z)
    a = _sigmoid(a0_ref[...] + _dot(wa, a2_ref[...]))
    g_ref[...] = _dot(_sigmoid(gd), g2_ref[...])
    kk = k * kk_ref[...]
    ss = _segsum(kk * kk, bd_ref[...])
    kk = kk * lax.rsqrt(jnp.maximum(ss, 1e-24))
    r_ref[...] = r
    k_ref[...] = k * (1.0 + (a - 1.0) * ka_ref[...])
    v_ref[...] = v
    a_ref[...] = -kk
    b_ref[...] = kk * a


def rwkv_prep(p, shift0, seq_len, lpk, tm):
    m, wr = p.shape
    c = lpk["c_rwkv"]
    tm = _row_tile(m, tm)
    row = lambda n: pl.BlockSpec((tm, n), lambda i: (i, 0))
    if seq_len == 1:
        tiles_per_seq, sh, sh_spec = 0, shift0, row(wr)
    else:
        assert seq_len % tm == 0 and tm % SUBLANES == 0
        tiles_per_seq = seq_len // tm
        sh = shift0.reshape(-1, 1, wr)
        sh_spec = pl.BlockSpec((1, 1, wr), lambda i: (i // tiles_per_seq, 0, 0))
    halo = pl.BlockSpec((SUBLANES, wr), lambda i: (jnp.maximum(i * (tm // SUBLANES) - 1, 0), 0))
    names = ("tshift_mu", "decay_w0", "decay_w2p", "iclr_a0", "iclr_a2p", "gate_g2", "k_k", "k_a", "bd")
    params = [lpk[n] for n in names]
    return pl.pallas_call(
        functools.partial(_rwkv_prep_kernel, c=c, tiles_per_seq=tiles_per_seq),
        grid=(m // tm,),
        in_specs=[row(wr), halo, sh_spec] + [_full(a.shape) for a in params],
        out_specs=[row(c)] * 7,
        out_shape=[jax.ShapeDtypeStruct((m, c), F32)] * 7,
        compiler_params=_cparams("parallel"),
        name="rwkv_prep",
    )(p, p, sh, *params)


def _wkv_chunk_kernel(r_ref, lw_ref, k_ref, v_ref, a_ref, b_ref, y_ref, s_ref, g_scr, *, L, npairs, nb):
    ci = pl.program_id(1)

    @pl.when(ci == 0)
    def _():
        g_scr[...] = jnp.zeros_like(g_scr)

    row = lax.broadcasted_iota(jnp.int32, (L, L), 0)
    col = lax.broadcasted_iota(jnp.int32, (L, L), 1)
    ltri = jnp.where(col <= row, 1.0, 0.0).astype(BF16)
    d = functools.partial(jnp.dot, preferred_element_type=F32)
    m0 = lax.broadcasted_iota(jnp.int32, (L, LANES), 1) < HEAD_SIZE
    i2 = lax.broadcasted_iota(jnp.int32, (2 * L, 2 * L), 0)
    j2 = lax.broadcasted_iota(jnp.int32, (2 * L, 2 * L), 1)
    same = (i2 < L) == (j2 < L)
    tril_s = same & (j2 < i2)
    tril_i = same & (j2 <= i2)
    eye = jnp.where(i2 == j2, 1.0, 0.0)

    def stack(x, p):
        xp = x[:, p * LANES:(p + 1) * LANES]
        return jnp.concatenate([jnp.where(m0, xp, 0.0), jnp.where(m0, 0.0, xp)], axis=0)

    inst = [(j, p) for j in range(nb) for p in range(npairs)]
    lhs, rhs, vm, p_last = [], [], [], []
    for j in range(nb):
        lw = lw_ref[j]
        hi, mid, lo = _split3(lw)
        cs = d(ltri, hi) + (d(ltri, mid) + d(ltri, lo))
        pin = jnp.exp(cs)
        pinv = jnp.exp(-cs)
        at = a_ref[j] * jnp.exp(cs - lw)
        rt = r_ref[j] * pin
        bt = b_ref[j] * pinv
        kt = k_ref[j] * pinv
        v = v_ref[j]
        for p in range(npairs):
            lhs.append(jnp.concatenate([stack(at, p), stack(rt, p)], axis=0).astype(BF16))
            rhs.append(jnp.concatenate([stack(bt, p), stack(kt, p)], axis=0).astype(BF16))
            vm.append(stack(v, p).astype(BF16))
            p_last.append(pin[L - 1:L, p * LANES:(p + 1) * LANES])
    n = len(inst)
    gm = [_dot_nt(lhs[i], rhs[i]) for i in range(n)]
    a_ab = [jnp.where(tril_s, gm[i][:2 * L, :2 * L], 0.0) for i in range(n)]
    a_kr = [jnp.concatenate([jnp.where(tril_s, gm[i][:2 * L, 2 * L:], 0.0),
                             jnp.where(tril_i, gm[i][2 * L:, 2 * L:], 0.0)], axis=0) for i in range(n)]
    a_rb = [jnp.where(tril_i, gm[i][2 * L:, :2 * L], 0.0) for i in range(n)]
    mv = [_dot(a_kr[i], vm[i]) for i in range(n)]
    g = [g_scr[j, p] for (j, p) in inst]
    gh = [_dot_nt(lhs[i], g[i]) for i in range(n)]
    t = [eye + a_ab[i] for i in range(n)]
    ak = [_dot(a_ab[i], a_ab[i]) for i in range(n)]
    for _ in range(int(math.log2(L)) - 2):
        pr = [_dot(ak[i], jnp.concatenate([ak[i], t[i]], axis=1)) for i in range(n)]
        ak = [pr[i][:, :2 * L] for i in range(n)]
        t = [t[i] + pr[i][:, 2 * L:] for i in range(n)]
    t = [t[i] + _dot(ak[i], t[i]) for i in range(n)]
    tc = [jnp.concatenate([t[i], _dot(a_rb[i], t[i])], axis=0) for i in range(n)]
    uy = [_dot(tc[i], gh[i][:2 * L] + mv[i][:2 * L]) for i in range(n)]
    for i, (j, p) in enumerate(inst):
        yy = gh[i][2 * L:] + mv[i][2 * L:] + uy[i][2 * L:]
        y_ref[j, :, p * LANES:(p + 1) * LANES] = yy[:L] + yy[L:]
        uv = jnp.concatenate([uy[i][:2 * L], vm[i].astype(F32)], axis=0)
        upd = _dot(uv.T, rhs[i])
        g_scr[j, p] = (g[i] + upd) * p_last[i]

    @pl.when(ci == pl.num_programs(1) - 1)
    def _():
        s_ref[...] = g_scr[...]


def wkv_chunked(r, lw, k, v, a, b, nb=1):
    bsz, t, c = r.shape
    L = WKV_CHUNK
    assert t % L == 0 and c % LANES == 0 and bsz % nb == 0
    npairs = c // LANES
    seq = pl.BlockSpec((nb, L, c), lambda bi, ci: (bi, ci, 0))
    y, s = pl.pallas_call(
        functools.partial(_wkv_chunk_kernel, L=L, npairs=npairs, nb=nb),
        grid=(bsz // nb, t // L),
        in_specs=[seq] * 6,
        out_specs=[seq, pl.BlockSpec((nb, npairs, LANES, LANES), lambda bi, ci: (bi, 0, 0, 0))],
        out_shape=[jax.ShapeDtypeStruct((bsz, t, c), F32),
                   jax.ShapeDtypeStruct((bsz, npairs, LANES, LANES), F32)],
        scratch_shapes=[pltpu.VMEM((nb, npairs, LANES, LANES), F32)],
        compiler_params=_cparams("parallel", "arbitrary"),
        name="wkv_chunked",
    )(r, lw, k, v, a, b)
    n = HEAD_SIZE
    s = jnp.stack([s[:, :, :n, :n], s[:, :, n:, n:]], axis=2)
    return y, s.reshape(bsz, 2 * npairs, n, n)


def _wkv_step_kernel(s_ref, r_ref, lw_ref, k_ref, v_ref, a_ref, b_ref, so_ref, y_ref, *, bb, nh):
    n = HEAD_SIZE
    eye = jnp.where(lax.broadcasted_iota(jnp.int32, (n, n), 0) == lax.broadcasted_iota(jnp.int32, (n, n), 1),
                    1.0, 0.0)

    def body(i, carry):
        r, k, v, a, b = r_ref[i], k_ref[i], v_ref[i], a_ref[i], b_ref[i]
        w = jnp.exp(lw_ref[i])
        rows = []
        for h in range(nh):
            hs = slice(h, h + 1)
            s = s_ref[i, h]
            sa = jnp.sum(s * a[hs], axis=1, keepdims=True)
            v_col = jnp.sum(eye * v[hs], axis=1, keepdims=True)
            s = s * w[hs] + sa * b[hs] + v_col * k[hs]
            so_ref[i, h] = s
            y_col = jnp.sum(s * r[hs], axis=1, keepdims=True)
            rows.append(jnp.sum(eye * y_col, axis=0, keepdims=True))
        y_ref[i] = jnp.concatenate(rows, axis=0)
        return carry

    lax.fori_loop(0, bb, body, 0)


def wkv_step(s0, r, lw, k, v, a, b):
    bsz, nh, n, _ = s0.shape
    bb = _row_tile(bsz, 8)
    vec = pl.BlockSpec((bb, nh, n), lambda i: (i, 0, 0))
    st = pl.BlockSpec((bb, nh, n, n), lambda i: (i, 0, 0, 0))
    hv = lambda x: x.reshape(bsz, nh, n)
    s1, y = pl.pallas_call(
        functools.partial(_wkv_step_kernel, bb=bb, nh=nh),
        grid=(bsz // bb,),
        in_specs=[st] + [vec] * 6,
        out_specs=[st, vec],
        out_shape=[jax.ShapeDtypeStruct(s0.shape, F32), jax.ShapeDtypeStruct((bsz, nh, n), F32)],
        compiler_params=_cparams("parallel"),
        name="wkv_step",
    )(s0, hv(r), hv(lw), hv(k), hv(v), hv(a), hv(b))
    return y.reshape(bsz, nh * n), s1


CONV_PAD = 32


def _glu(u, gb_ref, c):
    u = u + gb_ref[...]
    return u[:, :c] * _sigmoid(u[:, c:])


def _ln_swish(y, g_ref, b_ref):
    mu = jnp.mean(y, axis=-1, keepdims=True)
    dlt = y - mu
    var = jnp.mean(dlt * dlt, axis=-1, keepdims=True)
    y = dlt * lax.rsqrt(var + LN_EPS) * g_ref[...] + b_ref[...]
    return y * _sigmoid(y)


def _conv_seq_kernel(u_ref, buf_ref, gb_ref, cw_ref, cb_ref, lg_ref, lb_ref, o_ref, nb_ref, xp_scr, *, tt, c):
    ti = pl.program_id(1)

    @pl.when(ti == 0)
    def _():
        xp_scr[0:CONV_PAD, :] = buf_ref[0]

    xp_scr[CONV_PAD:CONV_PAD + tt, :] = _glu(u_ref[0], gb_ref, c)
    base = CONV_PAD - (CONV_WIDTH - 1)
    acc = jnp.zeros((tt, c), F32) + cb_ref[...]
    for w in range(CONV_WIDTH):
        acc = acc + xp_scr[base + w:base + w + tt, :] * cw_ref[w:w + 1, :]
    o_ref[0] = _ln_swish(acc, lg_ref, lb_ref)
    tail = xp_scr[tt:tt + CONV_PAD, :]
    xp_scr[0:CONV_PAD, :] = tail

    @pl.when(ti == pl.num_programs(1) - 1)
    def _():
        nb_ref[0] = tail


def conv_seq(u, buf0, lpk, tt):
    bsz, t, c2 = u.shape
    c = c2 // 2
    tt = _row_tile(t, tt)
    pad = CONV_PAD - (CONV_WIDTH - 1)
    bufp = jnp.pad(buf0, ((0, 0), (pad, 0), (0, 0)))
    params = [lpk[n] for n in ("glu_b", "conv_w", "conv_b", "conv_ln_g", "conv_ln_b")]
    hist = pl.BlockSpec((1, CONV_PAD, c), lambda bi, ti: (bi, 0, 0))
    out, nb = pl.pallas_call(
        functools.partial(_conv_seq_kernel, tt=tt, c=c),
        grid=(bsz, t // tt),
        in_specs=[pl.BlockSpec((1, tt, c2), lambda bi, ti: (bi, ti, 0)), hist] + [_full(a.shape) for a in params],
        out_specs=[pl.BlockSpec((1, tt, c), lambda bi, ti: (bi, ti, 0)), hist],
        out_shape=[jax.ShapeDtypeStruct((bsz, t, c), F32), jax.ShapeDtypeStruct((bsz, CONV_PAD, c), F32)],
        scratch_shapes=[pltpu.VMEM((CONV_PAD + tt, c), F32)],
        compiler_params=_cparams("parallel", "arbitrary"),
        name="conv_seq",
    )(u, bufp, *params)
    return out, nb[:, pad:]


def _conv_step_kernel(u_ref, buf_ref, gb_ref, cw_ref, cb_ref, lg_ref, lb_ref, o_ref, nb_ref, *, bb, c):
    hist = CONV_WIDTH - 1
    glu = _glu(u_ref[...], gb_ref, c)
    w_hist = cw_ref[0:hist, :]
    w_last = cw_ref[hist:hist + 1, :]
    rows = []
    for i in range(bb):
        rows.append(jnp.sum(buf_ref[i] * w_hist, axis=0, keepdims=True))
    acc = jnp.concatenate(rows, axis=0) + glu * w_last + cb_ref[...]
    o_ref[...] = _ln_swish(acc, lg_ref, lb_ref)
    for i in range(bb):
        nb_ref[i, 0:hist - 1, :] = buf_ref[i, 1:hist, :]
        nb_ref[i, hist - 1:hist, :] = glu[i:i + 1, :]


def conv_step(u, buf0, lpk):
    bsz, c2 = u.shape
    c = c2 // 2
    bb = _row_tile(bsz, 8)
    params = [lpk[n] for n in ("glu_b", "conv_w", "conv_b", "conv_ln_g", "conv_ln_b")]
    hist = pl.BlockSpec((bb, CONV_WIDTH - 1, c), lambda i: (i, 0, 0))
    return pl.pallas_call(
        functools.partial(_conv_step_kernel, bb=bb, c=c),
        grid=(bsz // bb,),
        in_specs=[pl.BlockSpec((bb, c2), lambda i: (i, 0)), hist] + [_full(a.shape) for a in params],
        out_specs=[pl.BlockSpec((bb, c), lambda i: (i, 0)), hist],
        out_shape=[jax.ShapeDtypeStruct((bsz, c), F32), jax.ShapeDtypeStruct(buf0.shape, F32)],
        compiler_params=_cparams("parallel"),
        name="conv_step",
    )(u, buf0, *params)


def _merge_kernel(x_ref, y_ref, r_ref, k_ref, v_ref, g_ref, cv_ref, pg_ref, bd_ref, lg_ref, lb_ref, rk_ref,
                  wa_ref, wp_ref, bp_ref, mb_ref, wo_ref, gp_ref, o_ref, *, d):
    bd = bd_ref[...]
    inv_n = 1.0 / HEAD_SIZE
    y = y_ref[...]
    mu = _segsum(y, bd) * inv_n
    dlt = y - mu
    var = _segsum(dlt * dlt, bd) * inv_n
    yn = dlt * lax.rsqrt(var + GN_EPS) * lg_ref[...] + lb_ref[...]
    v = v_ref[...]
    bonus = _segsum(r_ref[...] * k_ref[...] * rk_ref[...], bd) * v
    za = (yn + bonus) * g_ref[...]
    ya = _dot(za, wa_ref[...])
    yb = _dot(cv_ref[...], wp_ref[...]) + bp_ref[...]
    gate = _sigmoid(pg_ref[...] + mb_ref[...])
    mixed = _dot(gate[:, :d] * ya + gate[:, d:] * yb, wo_ref[...])
    o_ref[...] = x_ref[...] + _rms(mixed, gp_ref[...])


def merge(x, y, r, k, v, g, cv, pg, lpk, tm):
    m, d = x.shape
    c = y.shape[1]
    tm = _row_tile(m, tm)
    row = lambda n: pl.BlockSpec((tm, n), lambda i: (i, 0))
    names = ("bd", "lnx_g", "lnx_b", "r_k", "w_out_rwkv", "w_pw", "b_pw", "merge_gate_b", "w_out", "g_mix_post")
    params = [lpk[n] for n in names]
    return pl.pallas_call(
        functools.partial(_merge_kernel, d=d),
        grid=(m // tm,),
        in_specs=[row(d)] + [row(c)] * 6 + [row(2 * d)] + [_full(a.shape) for a in params],
        out_specs=row(d),
        out_shape=jax.ShapeDtypeStruct((m, d), F32),
        compiler_params=_cparams("parallel"),
        name="merge",
    )(x, y, r, k, v, g, cv, pg, *params)


def _xattn_kernel(x_ref, mk_ref, mv_ref, gq_ref, wq_ref, wo_ref, gp_ref, o_ref, *, dh):
    x = x_ref[0]
    q = _dot(_rms(x, gq_ref[...]), wq_ref[...])
    mk = mk_ref[0].astype(BF16)
    mv = mv_ref[0].astype(BF16)
    scale = dh ** -0.5
    outs = []
    for h in range(N_XHEADS):
        sl = slice(h * dh, (h + 1) * dh)
        s = _dot_nt(q[:, sl], mk[:, sl]) * scale
        s = s - jnp.max(s, axis=-1, keepdims=True)
        e = jnp.exp(s)
        pr = e / jnp.sum(e, axis=-1, keepdims=True)
        outs.append(_dot(pr, mv[:, sl]))
    o = jnp.concatenate(outs, axis=1)
    o_ref[0] = x + _rms(_dot(o, wo_ref[...]), gp_ref[...])


def xattn(x, mk, mv, lpk, tq):
    bsz, t, d = x.shape
    nm = mk.shape[1]
    tq = _row_tile(t, tq)
    params = [lpk[n] for n in ("g_x_pre", "w_q", "w_o_x", "g_x_post")]
    xs = pl.BlockSpec((1, tq, d), lambda bi, ti: (bi, ti, 0))
    ms = pl.BlockSpec((1, nm, d), lambda bi, ti: (bi, 0, 0))
    return pl.pallas_call(
        functools.partial(_xattn_kernel, dh=d // N_XHEADS),
        grid=(bsz, t // tq),
        in_specs=[xs, ms, ms] + [_full(a.shape) for a in params],
        out_specs=xs,
        out_shape=jax.ShapeDtypeStruct(x.shape, F32),
        compiler_params=_cparams("parallel", "arbitrary"),
        name="xattn",
    )(x, mk, mv, *params)


def _xattn_step_kernel(q_ref, k_ref, v_ref, o_ref, *, dh):
    q = q_ref[0]
    s = jnp.sum(k_ref[0] * q[None, :, :], axis=-1, keepdims=True) * dh ** -0.5
    s = s - jnp.max(s, axis=0, keepdims=True)
    e = jnp.exp(s)
    pr = e / jnp.sum(e, axis=0, keepdims=True)
    o_ref[0] = jnp.sum(pr * v_ref[0], axis=0)


def xattn_step(q, cache_k, cache_v, layer):
    bsz, d = q.shape
    _, _, nm, nh, dh = cache_k.shape
    qs = pl.BlockSpec((1, nh, dh), lambda i: (i, 0, 0))
    cs = pl.BlockSpec((None, 1, nm, nh, dh), lambda i: (layer, i, 0, 0, 0))
    o = pl.pallas_call(
        functools.partial(_xattn_step_kernel, dh=dh),
        grid=(bsz,),
        in_specs=[qs, cs, cs],
        out_specs=qs,
        out_shape=jax.ShapeDtypeStruct((bsz, nh, dh), F32),
        compiler_params=_cparams("parallel"),
        name="xattn_step",
    )(q.reshape(bsz, nh, dh), cache_k, cache_v)
    return o.reshape(bsz, d)


def _mm_norm_res_kernel(a_ref, x_ref, w_ref, g_ref, o_ref):
    o_ref[...] = x_ref[...] + _rms(_dot(a_ref[...], w_ref[...]), g_ref[...])


def mm_norm_res(a, x, w, g, tm):
    m, k = a.shape
    n = w.shape[1]
    tm = _row_tile(m, tm)
    return pl.pallas_call(
        _mm_norm_res_kernel,
        grid=(m // tm,),
        in_specs=[pl.BlockSpec((tm, k), lambda i: (i, 0)), pl.BlockSpec((tm, n), lambda i: (i, 0)),
                  _full((k, n)), _full((1, n))],
        out_specs=pl.BlockSpec((tm, n), lambda i: (i, 0)),
        out_shape=jax.ShapeDtypeStruct((m, n), F32),
        compiler_params=_cparams("parallel"),
        name="mm_norm_res",
    )(a, x, w, g)


def _ffn_kernel(x_ref, gpre_ref, wg_ref, wu_ref, wd_ref, gpost_ref, o_ref, h_scr, acc_scr):
    f = pl.program_id(1)

    @pl.when(f == 0)
    def _():
        h_scr[...] = _rms(x_ref[...], gpre_ref[...]).astype(BF16)
        acc_scr[...] = jnp.zeros_like(acc_scr)

    h = h_scr[...]
    gt = jnp.dot(h, wg_ref[...], preferred_element_type=F32)
    up = jnp.dot(h, wu_ref[...], preferred_element_type=F32)
    acc_scr[...] += _dot(gt * _sigmoid(gt) * up, wd_ref[...])

    @pl.when(f == pl.num_programs(1) - 1)
    def _():
        o_ref[...] = x_ref[...] + _rms(acc_scr[...], gpost_ref[...])


def ffn(x, lpk, tm):
    m, d = x.shape
    w_gu, w_dn = lpk["w_gate_up"], lpk["w_down"]
    dff = w_dn.shape[0]
    nf = 2
    fc = dff // nf
    assert fc * nf == dff and fc % LANES == 0
    tm = _row_tile(m, tm)
    return pl.pallas_call(
        _ffn_kernel,
        grid=(m // tm, nf),
        in_specs=[pl.BlockSpec((tm, d), lambda i, f: (i, 0)), _full((1, d)),
                  pl.BlockSpec((d, fc), lambda i, f: (0, f)),
                  pl.BlockSpec((d, fc), lambda i, f: (0, f + nf)),
                  pl.BlockSpec((fc, d), lambda i, f: (f, 0)), _full((1, d))],
        out_specs=pl.BlockSpec((tm, d), lambda i, f: (i, 0)),
        out_shape=jax.ShapeDtypeStruct((m, d), F32),
        scratch_shapes=[pltpu.VMEM((tm, d), BF16), pltpu.VMEM((tm, d), F32)],
        compiler_params=_cparams("parallel", "arbitrary"),
        name="ffn",
    )(x, lpk["g_ffn_pre"], w_gu, w_gu, w_dn, lpk["g_ffn_post"])


def _layer_params(l, w):
    row = lambda a: a[l].reshape(1, -1)
    c = w["w_out_rwkv"].shape[1]
    lora_rows = D_DECAY_LORA + D_ICLR_LORA
    w2p = jnp.zeros((lora_rows, c), F32).at[:D_DECAY_LORA].set(w["decay_w2"][l])
    a2p = jnp.zeros((lora_rows, c), F32).at[D_DECAY_LORA:].set(w["iclr_a2"][l])
    head = jnp.arange(c) // HEAD_SIZE
    lpk = {
        "c_rwkv": c,
        "bd": (head[:, None] == head[None, :]).astype(BF16),
        "decay_w2p": w2p.astype(BF16), "iclr_a2p": a2p.astype(BF16),
        "conv_w": jnp.pad(w["conv_w"][l], ((0, CONV_PAD - CONV_WIDTH), (0, 0))),
    }
    for n in ("w_in", "gate_g2", "w_out_rwkv", "w_pw", "w_out", "w_q", "w_o_x", "w_gate_up", "w_down"):
        lpk[n] = w[n][l].astype(BF16)
    lpk["w_kv"] = jnp.concatenate([w["w_k"][l], w["w_v"][l]], axis=1).astype(BF16)
    for n in ("tshift_mu", "decay_w0", "iclr_a0", "k_k", "k_a", "r_k", "lnx_g", "lnx_b", "glu_b", "conv_b",
              "conv_ln_g", "conv_ln_b", "b_pw", "merge_gate_b", "g_mix_pre", "g_mix_post", "g_mem", "g_x_pre",
              "g_x_post", "g_ffn_pre", "g_ffn_post"):
        lpk[n] = row(w[n])
    return lpk


def _decoder_layer(x, mem_k, mem_v, s0, shift0, buf0, lpk, tm, mem_layer=0):
    bsz, t, d = x.shape
    m = bsz * t
    c = lpk["c_rwkv"]
    wr = 3 * c + D_DECAY_LORA + D_ICLR_LORA + D_GATE_LORA
    x2 = x.reshape(m, d)
    p_rwkv, p_conv, p_gate = norm_mm(x2, lpk["g_mix_pre"], lpk["w_in"], (wr, 2 * c, 2 * d), tm)
    p3 = p_rwkv.reshape(bsz, t, wr)
    r, lw, k, v, a, b, g = rwkv_prep(p_rwkv, shift0, t, lpk, tm)
    if s0 is None:
        seq = lambda z: z.reshape(bsz, t, c)
        y, s1 = wkv_chunked(seq(r), seq(lw), seq(k), seq(v), seq(a), seq(b), nb=2 if bsz % 2 == 0 else 1)
        y = y.reshape(m, c)
        cv, buf1 = conv_seq(p_conv.reshape(bsz, t, 2 * c), buf0, lpk, 256)
        cv = cv.reshape(m, c)
    else:
        y, s1 = wkv_step(s0, r, lw, k, v, a, b)
        cv, buf1 = conv_step(p_conv, buf0, lpk)
    x2 = merge(x2, y, r, k, v, g, cv, p_gate, lpk, tm)
    if s0 is not None:
        (q,) = norm_mm(x2, lpk["g_x_pre"], lpk["w_q"], (d,), tm)
        o = xattn_step(q, mem_k, mem_v, mem_layer)
        x2 = mm_norm_res(o, x2, lpk["w_o_x"], lpk["g_x_post"], tm)
    else:
        x2 = xattn(x2.reshape(bsz, t, d), mem_k, mem_v, lpk, 256).reshape(m, d)
    x2 = ffn(x2, lpk, 2 * tm)
    return x2.reshape(bsz, t, d), s1, p3[:, -1], buf1


def kernel(x_prompt, x_sample, mem_prompt, state_rwkv, state_shift, state_conv, cache_mem_k, cache_mem_v, w_in, tshift_mu, decay_w0, decay_w2, iclr_a0, iclr_a2, gate_g2, k_k, k_a, r_k, lnx_g, lnx_b, w_out_rwkv, glu_b, conv_w, conv_b, conv_ln_g, conv_ln_b, w_pw, b_pw, merge_gate_b, w_out, g_mix_pre, g_mix_post, g_mem, w_q, w_k, w_v, w_o_x, g_x_pre, g_x_post, g_ffn_pre, g_ffn_post, w_gate_up, w_down):
    w = dict(w_in=w_in, tshift_mu=tshift_mu, decay_w0=decay_w0, decay_w2=decay_w2, iclr_a0=iclr_a0,
             iclr_a2=iclr_a2, gate_g2=gate_g2, k_k=k_k, k_a=k_a, r_k=r_k, lnx_g=lnx_g, lnx_b=lnx_b,
             w_out_rwkv=w_out_rwkv, glu_b=glu_b, conv_w=conv_w, conv_b=conv_b, conv_ln_g=conv_ln_g,
             conv_ln_b=conv_ln_b, w_pw=w_pw, b_pw=b_pw, merge_gate_b=merge_gate_b, w_out=w_out,
             g_mix_pre=g_mix_pre, g_mix_post=g_mix_post, g_mem=g_mem, w_q=w_q, w_k=w_k, w_v=w_v, w_o_x=w_o_x,
             g_x_pre=g_x_pre, g_x_post=g_x_post, g_ffn_pre=g_ffn_pre, g_ffn_post=g_ffn_post,
             w_gate_up=w_gate_up, w_down=w_down)
    depth = w_in.shape[0]
    bp, _, d = x_prompt.shape
    bs = x_sample.shape[0]
    nm = mem_prompt.shape[1]
    c = w_out_rwkv.shape[1]
    wr = state_shift.shape[-1]
    xp, xs = x_prompt, x_sample
    mem2 = mem_prompt.reshape(bp * nm, d)
    outs = {n: [] for n in ("rwkv_p", "shift_p", "conv_p", "memk_p", "memv_p", "rwkv_s", "shift_s", "conv_s")}
    for l in range(depth):
        lpk = _layer_params(l, w)
        mk, mv = norm_mm(mem2, lpk["g_mem"], lpk["w_kv"], (d, d), 256)
        mk, mv = mk.reshape(bp, nm, d), mv.reshape(bp, nm, d)
        xp, s_p, sh_p, cb_p = _decoder_layer(xp, mk, mv, None, jnp.zeros((bp, wr), F32),
                                             jnp.zeros((bp, CONV_WIDTH - 1, c), F32), lpk, 256)
        xs, s_s, sh_s, cb_s = _decoder_layer(xs, cache_mem_k, cache_mem_v, state_rwkv[l], state_shift[l],
                                             state_conv[l], lpk, 128, mem_layer=l)
        dh = d // N_XHEADS
        outs["rwkv_p"].append(s_p); outs["shift_p"].append(sh_p); outs["conv_p"].append(cb_p)
        outs["memk_p"].append(mk.reshape(bp, nm, N_XHEADS, dh)); outs["memv_p"].append(mv.reshape(bp, nm, N_XHEADS, dh))
        outs["rwkv_s"].append(s_s); outs["shift_s"].append(sh_s); outs["conv_s"].append(cb_s)
    st = {n: jnp.stack(v) for n, v in outs.items()}
    return (xp, xs, st["rwkv_p"], st["shift_p"], st["conv_p"], st["memk_p"], st["memv_p"],
            st["rwkv_s"], st["shift_s"], st["conv_s"])
```

```python
import functools
import math

import jax
import jax.numpy as jnp
from jax import lax
from jax.experimental import pallas as pl
from jax.experimental.pallas import tpu as pltpu

F32 = jnp.float32
BF16 = jnp.bfloat16

HEAD_SIZE = 64
D_DECAY_LORA = 64
D_ICLR_LORA = 64
D_GATE_LORA = 128
CONV_WIDTH = 31
N_XHEADS = 4
GN_EPS = 64e-5
NORM_EPS = 1e-6
LN_EPS = 1e-5

LANES = 128
SUBLANES = 8
VMEM_LIMIT_BYTES = 56 * 1024 * 1024

WKV_CHUNK = 64


def _cparams(*sem):
    return pltpu.CompilerParams(dimension_semantics=sem, vmem_limit_bytes=VMEM_LIMIT_BYTES)


def _dot(a, b):
    return jnp.dot(a.astype(BF16), b.astype(BF16), preferred_element_type=F32)


def _dot_nt(a, b):
    return lax.dot_general(a.astype(BF16), b.astype(BF16), (((1,), (1,)), ((), ())),
                           preferred_element_type=F32)


def _split3(x):
    hi = x.astype(BF16)
    r1 = x - hi.astype(F32)
    mid = r1.astype(BF16)
    lo = (r1 - mid.astype(F32)).astype(BF16)
    return hi, mid, lo


def _segsum(x, bd):
    return jnp.dot(x.astype(BF16), bd, preferred_element_type=F32)


def _rms(x, g):
    return x * lax.rsqrt(jnp.mean(x * x, axis=-1, keepdims=True) + NORM_EPS) * g


def _sigmoid(x):
    return 1.0 / (1.0 + jnp.exp(-x))


def _row_tile(m, target):
    t = min(m, target)
    assert m % t == 0, (m, t)
    return t


def _full(shape):
    nd = len(shape)
    return pl.BlockSpec(shape, lambda *_: (0,) * nd)


def _norm_mm_kernel(x_ref, g_ref, w_ref, *o_refs, splits):
    h = _rms(x_ref[...], g_ref[...]).astype(BF16)
    off = 0
    for o_ref, n in zip(o_refs, splits):
        o_ref[...] = jnp.dot(h, w_ref[:, off:off + n], preferred_element_type=F32)
        off += n


def norm_mm(x, g, w, splits, tm):
    m, k = x.shape
    tm = _row_tile(m, tm)
    n_all = w.shape[1]
    assert sum(splits) == n_all and all(s % LANES == 0 for s in splits)
    return pl.pallas_call(
        functools.partial(_norm_mm_kernel, splits=tuple(splits)),
        grid=(m // tm,),
        in_specs=[pl.BlockSpec((tm, k), lambda i: (i, 0)), _full((1, k)), _full((k, n_all))],
        out_specs=[pl.BlockSpec((tm, n), lambda i: (i, 0)) for n in splits],
        out_shape=[jax.ShapeDtypeStruct((m, n), F32) for n in splits],
        compiler_params=_cparams("parallel"),
        name="norm_mm",
    )(x, g.reshape(1, k), w)


def _proj_in_kernel(x_ref, sh_ref, gpre_ref, w_ref, mu_ref, w0_ref, w2_ref, a0_ref, a2_ref, g2_ref, kk_ref,
                    ka_ref, bd_ref, gb_ref, mb_ref, r_ref, lw_ref, k_ref, v_ref, a_ref, b_ref, g_ref, glu_ref,
                    gate_ref, last_ref, prev_scr, *, c, tiles_per_seq):
    h = _rms(x_ref[...], gpre_ref[...]).astype(BF16)
    wr = 3 * c + D_DECAY_LORA + D_ICLR_LORA + D_GATE_LORA
    p = jnp.dot(h, w_ref[:, :wr], preferred_element_type=F32)
    u = jnp.dot(h, w_ref[:, wr:wr + 2 * c], preferred_element_type=F32) + gb_ref[...]
    glu_ref[...] = u[:, :c] * _sigmoid(u[:, c:])
    gate_ref[...] = _sigmoid(jnp.dot(h, w_ref[:, wr + 2 * c:], preferred_element_type=F32) + mb_ref[...])
    if tiles_per_seq == 0:
        pp = sh_ref[...]
        last_ref[...] = p
    else:
        tm = p.shape[0]

        @pl.when(pl.program_id(0) == 0)
        def _():
            prev_scr[...] = jnp.zeros_like(prev_scr)

        first = (pl.program_id(0) % tiles_per_seq) == 0
        prev_row = jnp.where(first, sh_ref[0], prev_scr[...])
        rows = lax.broadcasted_iota(jnp.int32, p.shape, 0)
        pp = jnp.where(rows == 0, prev_row, pltpu.roll(p, 1, axis=0))
        prev_scr[...] = p[tm - 1:tm, :]
        last_ref[0] = p[tm - 1:tm, :]
    xs = p + (pp - p) * mu_ref[...]
    r = xs[:, 0:c]
    k = xs[:, c:2 * c]
    v = xs[:, 2 * c:3 * c]
    wa = xs[:, 3 * c:3 * c + D_DECAY_LORA + D_ICLR_LORA]
    gd = xs[:, 3 * c + D_DECAY_LORA + D_ICLR_LORA:]
    z = w0_ref[...] + _dot(jnp.tanh(wa), w2_ref[...])
    lw_ref[...] = -math.exp(-0.5) * _sigmoid(z)
    a = _sigmoid(a0_ref[...] + _dot(wa, a2_ref[...]))
    g_ref[...] = _dot(_sigmoid(gd), g2_ref[...])
    kk = k * kk_ref[...]
    ss = _segsum(kk * kk, bd_ref[...])
    kk = kk * lax.rsqrt(jnp.maximum(ss, 1e-24))
    r_ref[...] = r
    k_ref[...] = k * (1.0 + (a - 1.0) * ka_ref[...])
    v_ref[...] = v
    a_ref[...] = -kk
    b_ref[...] = kk * a


def proj_in(x, shift0, seq_len, lpk, tm):
    m, d = x.shape
    c = lpk["c_rwkv"]
    wr = shift0.shape[-1]
    nseq = m // seq_len
    tm = _row_tile(m, tm)
    row = lambda n: pl.BlockSpec((tm, n), lambda i: (i, 0))
    if seq_len == 1:
        tiles_per_seq, sh, sh_spec = 0, shift0, row(wr)
        last_spec, last_shape = row(wr), (m, wr)
    else:
        assert seq_len % tm == 0
        tiles_per_seq = seq_len // tm
        sh = shift0.reshape(nseq, 1, wr)
        sh_spec = pl.BlockSpec((1, 1, wr), lambda i: (i // tiles_per_seq, 0, 0))
        last_spec, last_shape = sh_spec, (nseq, 1, wr)
    names = ("g_mix_pre", "w_in", "tshift_mu", "decay_w0", "decay_w2p", "iclr_a0", "iclr_a2p", "gate_g2", "k_k",
             "k_a", "bd", "glu_b", "merge_gate_b")
    params = [lpk[n] for n in names]
    outs = pl.pallas_call(
        functools.partial(_proj_in_kernel, c=c, tiles_per_seq=tiles_per_seq),
        grid=(m // tm,),
        in_specs=[row(d), sh_spec] + [_full(a.shape) for a in params],
        out_specs=[row(c)] * 8 + [row(2 * d), last_spec],
        out_shape=[jax.ShapeDtypeStruct((m, c), F32)] * 8 + [jax.ShapeDtypeStruct((m, 2 * d), F32),
                                                             jax.ShapeDtypeStruct(last_shape, F32)],
        scratch_shapes=[pltpu.VMEM((1, wr), F32)],
        compiler_params=_cparams("arbitrary"),
        name="proj_in",
    )(x, sh, *params)
    return outs[:9], outs[9].reshape(nseq, wr)


def _wkv_chunk_kernel(r_ref, lw_ref, k_ref, v_ref, a_ref, b_ref, y_ref, s_ref, g_scr, *, L, npairs, nb):
    ci = pl.program_id(1)

    @pl.when(ci == 0)
    def _():
        g_scr[...] = jnp.zeros_like(g_scr)

    row = lax.broadcasted_iota(jnp.int32, (L, L), 0)
    col = lax.broadcasted_iota(jnp.int32, (L, L), 1)
    ltri = jnp.where(col <= row, 1.0, 0.0).astype(BF16)
    d = functools.partial(jnp.dot, preferred_element_type=F32)
    m0 = lax.broadcasted_iota(jnp.int32, (L, LANES), 1) < HEAD_SIZE
    i2 = lax.broadcasted_iota(jnp.int32, (2 * L, 2 * L), 0)
    j2 = lax.broadcasted_iota(jnp.int32, (2 * L, 2 * L), 1)
    same = (i2 < L) == (j2 < L)
    tril_s = same & (j2 < i2)
    tril_i = same & (j2 <= i2)
    eye = jnp.where(i2 == j2, 1.0, 0.0)

    def stack(x, p):
        xp = x[:, p * LANES:(p + 1) * LANES]
        return jnp.concatenate([jnp.where(m0, xp, 0.0), jnp.where(m0, 0.0, xp)], axis=0)

    inst = [(j, p) for j in range(nb) for p in range(npairs)]
    lhs, rhs, vm, p_last = [], [], [], []
    for j in range(nb):
        lw = lw_ref[j]
        hi, mid, lo = _split3(lw)
        cs = d(ltri, hi) + (d(ltri, mid) + d(ltri, lo))
        pin = jnp.exp(cs)
        pinv = jnp.exp(-cs)
        at = a_ref[j] * jnp.exp(cs - lw)
        rt = r_ref[j] * pin
        bt = b_ref[j] * pinv
        kt = k_ref[j] * pinv
        v = v_ref[j]
        for p in range(npairs):
            lhs.append(jnp.concatenate([stack(at, p), stack(rt, p)], axis=0).astype(BF16))
            rhs.append(jnp.concatenate([stack(bt, p), stack(kt, p)], axis=0).astype(BF16))
            vm.append(stack(v, p).astype(BF16))
            p_last.append(pin[L - 1:L, p * LANES:(p + 1) * LANES])
    n = len(inst)
    gm = [_dot_nt(lhs[i], rhs[i]) for i in range(n)]
    a_ab = [jnp.where(tril_s, gm[i][:2 * L, :2 * L], 0.0) for i in range(n)]
    a_kr = [jnp.concatenate([jnp.where(tril_s, gm[i][:2 * L, 2 * L:], 0.0),
                             jnp.where(tril_i, gm[i][2 * L:, 2 * L:], 0.0)], axis=0) for i in range(n)]
    a_rb = [jnp.where(tril_i, gm[i][2 * L:, :2 * L], 0.0) for i in range(n)]
    mv = [_dot(a_kr[i], vm[i]) for i in range(n)]
    g = [g_scr[j, p] for (j, p) in inst]
    gh = [_dot_nt(lhs[i], g[i]) for i in range(n)]
    t = [eye + a_ab[i] for i in range(n)]
    ak = [_dot(a_ab[i], a_ab[i]) for i in range(n)]
    for _ in range(int(math.log2(L)) - 2):
        pr = [_dot(ak[i], jnp.concatenate([ak[i], t[i]], axis=1)) for i in range(n)]
        ak = [pr[i][:, :2 * L] for i in range(n)]
        t = [t[i] + pr[i][:, 2 * L:] for i in range(n)]
    t = [t[i] + _dot(ak[i], t[i]) for i in range(n)]
    tc = [jnp.concatenate([t[i], _dot(a_rb[i], t[i])], axis=0) for i in range(n)]
    uy = [_dot(tc[i], gh[i][:2 * L] + mv[i][:2 * L]) for i in range(n)]
    for i, (j, p) in enumerate(inst):
        yy = gh[i][2 * L:] + mv[i][2 * L:] + uy[i][2 * L:]
        y_ref[j, :, p * LANES:(p + 1) * LANES] = yy[:L] + yy[L:]
        uv = jnp.concatenate([uy[i][:2 * L], vm[i].astype(F32)], axis=0)
        upd = _dot(uv.T, rhs[i])
        g_scr[j, p] = (g[i] + upd) * p_last[i]

    @pl.when(ci == pl.num_programs(1) - 1)
    def _():
        s_ref[...] = g_scr[...]


def wkv_chunked(r, lw, k, v, a, b, nb=1):
    bsz, t, c = r.shape
    L = WKV_CHUNK
    assert t % L == 0 and c % LANES == 0 and bsz % nb == 0
    npairs = c // LANES
    seq = pl.BlockSpec((nb, L, c), lambda bi, ci: (bi, ci, 0))
    y, s = pl.pallas_call(
        functools.partial(_wkv_chunk_kernel, L=L, npairs=npairs, nb=nb),
        grid=(bsz // nb, t // L),
        in_specs=[seq] * 6,
        out_specs=[seq, pl.BlockSpec((nb, npairs, LANES, LANES), lambda bi, ci: (bi, 0, 0, 0))],
        out_shape=[jax.ShapeDtypeStruct((bsz, t, c), F32),
                   jax.ShapeDtypeStruct((bsz, npairs, LANES, LANES), F32)],
        scratch_shapes=[pltpu.VMEM((nb, npairs, LANES, LANES), F32)],
        compiler_params=_cparams("parallel", "arbitrary"),
        name="wkv_chunked",
    )(r, lw, k, v, a, b)
    n = HEAD_SIZE
    s = jnp.stack([s[:, :, :n, :n], s[:, :, n:, n:]], axis=2)
    return y, s.reshape(bsz, 2 * npairs, n, n)


def _wkv_step_kernel(s_ref, r_ref, lw_ref, k_ref, v_ref, a_ref, b_ref, so_ref, y_ref, *, bb, nh):
    n = HEAD_SIZE
    eye = jnp.where(lax.broadcasted_iota(jnp.int32, (n, n), 0) == lax.broadcasted_iota(jnp.int32, (n, n), 1),
                    1.0, 0.0)

    def body(i, carry):
        r, k, v, a, b = r_ref[i], k_ref[i], v_ref[i], a_ref[i], b_ref[i]
        w = jnp.exp(lw_ref[i])
        rows = []
        for h in range(nh):
            hs = slice(h, h + 1)
            s = s_ref[i, h]
            sa = jnp.sum(s * a[hs], axis=1, keepdims=True)
            v_col = jnp.sum(eye * v[hs], axis=1, keepdims=True)
            s = s * w[hs] + sa * b[hs] + v_col * k[hs]
            so_ref[i, h] = s
            y_col = jnp.sum(s * r[hs], axis=1, keepdims=True)
            rows.append(jnp.sum(eye * y_col, axis=0, keepdims=True))
        y_ref[i] = jnp.concatenate(rows, axis=0)
        return carry

    lax.fori_loop(0, bb, body, 0)


def wkv_step(s0, r, lw, k, v, a, b):
    bsz, nh, n, _ = s0.shape
    bb = _row_tile(bsz, 8)
    vec = pl.BlockSpec((bb, nh, n), lambda i: (i, 0, 0))
    st = pl.BlockSpec((bb, nh, n, n), lambda i: (i, 0, 0, 0))
    hv = lambda x: x.reshape(bsz, nh, n)
    s1, y = pl.pallas_call(
        functools.partial(_wkv_step_kernel, bb=bb, nh=nh),
        grid=(bsz // bb,),
        in_specs=[st] + [vec] * 6,
        out_specs=[st, vec],
        out_shape=[jax.ShapeDtypeStruct(s0.shape, F32), jax.ShapeDtypeStruct((bsz, nh, n), F32)],
        compiler_params=_cparams("parallel"),
        name="wkv_step",
    )(s0, hv(r), hv(lw), hv(k), hv(v), hv(a), hv(b))
    return y.reshape(bsz, nh * n), s1


CONV_PAD = 32


CONV_ROWS = 32


def _ln_swish(y, g_ref, b_ref):
    mu = jnp.mean(y, axis=-1, keepdims=True)
    dlt = y - mu
    var = jnp.mean(dlt * dlt, axis=-1, keepdims=True)
    y = dlt * lax.rsqrt(var + LN_EPS) * g_ref[...] + b_ref[...]
    return y * _sigmoid(y)


def _conv_seq_kernel(u_ref, buf_ref, cw_ref, cb_ref, lg_ref, lb_ref, o_ref, nb_ref, xp_scr, *, tt):
    ti = pl.program_id(1)

    @pl.when(ti == 0)
    def _():
        xp_scr[0:CONV_PAD, :] = buf_ref[0]

    xp_scr[CONV_PAD:CONV_PAD + tt, :] = u_ref[0]
    base = CONV_PAD - (CONV_WIDTH - 1)
    rb = CONV_ROWS

    for i in range(tt // rb):
        s0 = i * rb
        acc = None
        for r in range(SUBLANES):
            rows = rb if r == 0 else rb + SUBLANES
            part = None
            for j in range((CONV_PAD + SUBLANES) // SUBLANES):
                w = SUBLANES * j + r - base
                if 0 <= w < CONV_WIDTH:
                    term = xp_scr[pl.ds(s0 + SUBLANES * j, rows), :] * cw_ref[w:w + 1, :]
                    part = term if part is None else part + term
            part = part if r == 0 else part[r:r + rb]
            acc = part if acc is None else acc + part
        o_ref[0, pl.ds(s0, rb), :] = _ln_swish(acc + cb_ref[...], lg_ref, lb_ref)
    tail = xp_scr[tt:tt + CONV_PAD, :]
    xp_scr[0:CONV_PAD, :] = tail

    @pl.when(ti == pl.num_programs(1) - 1)
    def _():
        nb_ref[0] = tail


def conv_seq(u, buf0, lpk, tt):
    bsz, t, c = u.shape
    tt = _row_tile(t, tt)
    assert tt % CONV_ROWS == 0
    pad = CONV_PAD - (CONV_WIDTH - 1)
    bufp = jnp.pad(buf0, ((0, 0), (pad, 0), (0, 0)))
    params = [lpk[n] for n in ("conv_w", "conv_b", "conv_ln_g", "conv_ln_b")]
    hist = pl.BlockSpec((1, CONV_PAD, c), lambda bi, ti: (bi, 0, 0))
    out, nb = pl.pallas_call(
        functools.partial(_conv_seq_kernel, tt=tt),
        grid=(bsz, t // tt),
        in_specs=[pl.BlockSpec((1, tt, c), lambda bi, ti: (bi, ti, 0)), hist] + [_full(a.shape) for a in params],
        out_specs=[pl.BlockSpec((1, tt, c), lambda bi, ti: (bi, ti, 0)), hist],
        out_shape=[jax.ShapeDtypeStruct((bsz, t, c), F32), jax.ShapeDtypeStruct((bsz, CONV_PAD, c), F32)],
        scratch_shapes=[pltpu.VMEM((CONV_PAD + tt, c), F32)],
        compiler_params=_cparams("parallel", "arbitrary"),
        name="conv_seq",
    )(u, bufp, *params)
    return out, nb[:, pad:]


def _conv_step_kernel(u_ref, buf_ref, cw_ref, cb_ref, lg_ref, lb_ref, o_ref, nb_ref, *, bb):
    hist = CONV_WIDTH - 1
    glu = u_ref[...]
    w_hist = cw_ref[0:hist, :]
    w_last = cw_ref[hist:hist + 1, :]
    rows = []
    for i in range(bb):
        rows.append(jnp.sum(buf_ref[i] * w_hist, axis=0, keepdims=True))
    acc = jnp.concatenate(rows, axis=0) + glu * w_last + cb_ref[...]
    o_ref[...] = _ln_swish(acc, lg_ref, lb_ref)
    for i in range(bb):
        nb_ref[i, 0:hist - 1, :] = buf_ref[i, 1:hist, :]
        nb_ref[i, hist - 1:hist, :] = glu[i:i + 1, :]


def conv_step(u, buf0, lpk):
    bsz, c = u.shape
    bb = _row_tile(bsz, 8)
    params = [lpk[n] for n in ("conv_w", "conv_b", "conv_ln_g", "conv_ln_b")]
    hist = pl.BlockSpec((bb, CONV_WIDTH - 1, c), lambda i: (i, 0, 0))
    return pl.pallas_call(
        functools.partial(_conv_step_kernel, bb=bb),
        grid=(bsz // bb,),
        in_specs=[pl.BlockSpec((bb, c), lambda i: (i, 0)), hist] + [_full(a.shape) for a in params],
        out_specs=[pl.BlockSpec((bb, c), lambda i: (i, 0)), hist],
        out_shape=[jax.ShapeDtypeStruct((bsz, c), F32), jax.ShapeDtypeStruct(buf0.shape, F32)],
        compiler_params=_cparams("parallel"),
        name="conv_step",
    )(u, buf0, *params)


def _merge_kernel(x_ref, y_ref, r_ref, k_ref, v_ref, g_ref, cv_ref, gate_ref, bd_ref, lg_ref, lb_ref, rk_ref,
                  wa_ref, wp_ref, bp_ref, wo_ref, gp_ref, o_ref, *, d):
    bd = bd_ref[...]
    inv_n = 1.0 / HEAD_SIZE
    y = y_ref[...]
    mu = _segsum(y, bd) * inv_n
    dlt = y - mu
    var = _segsum(dlt * dlt, bd) * inv_n
    yn = dlt * lax.rsqrt(var + GN_EPS) * lg_ref[...] + lb_ref[...]
    v = v_ref[...]
    bonus = _segsum(r_ref[...] * k_ref[...] * rk_ref[...], bd) * v
    za = (yn + bonus) * g_ref[...]
    ya = _dot(za, wa_ref[...])
    yb = _dot(cv_ref[...], wp_ref[...]) + bp_ref[...]
    mixed = _dot(gate_ref[:, :d] * ya + gate_ref[:, d:] * yb, wo_ref[...])
    o_ref[...] = x_ref[...] + _rms(mixed, gp_ref[...])


def merge(x, y, r, k, v, g, cv, gate, lpk, tm):
    m, d = x.shape
    c = y.shape[1]
    tm = _row_tile(m, tm)
    row = lambda n: pl.BlockSpec((tm, n), lambda i: (i, 0))
    names = ("bd", "lnx_g", "lnx_b", "r_k", "w_out_rwkv", "w_pw", "b_pw", "w_out", "g_mix_post")
    params = [lpk[n] for n in names]
    return pl.pallas_call(
        functools.partial(_merge_kernel, d=d),
        grid=(m // tm,),
        in_specs=[row(d)] + [row(c)] * 6 + [row(2 * d)] + [_full(a.shape) for a in params],
        out_specs=row(d),
        out_shape=jax.ShapeDtypeStruct((m, d), F32),
        compiler_params=_cparams("parallel"),
        name="merge",
    )(x, y, r, k, v, g, cv, gate, *params)


def _xattn_kernel(x_ref, mk_ref, mv_ref, gq_ref, wq_ref, wo_ref, gp_ref, o_ref, mk_scr, mv_scr, *, dh):
    @pl.when(pl.program_id(1) == 0)
    def _():
        mk_scr[...] = mk_ref[0].astype(BF16)
        mv_scr[...] = mv_ref[0].astype(BF16)

    x = x_ref[0]
    q = _dot(_rms(x, gq_ref[...]), wq_ref[...])
    scale = dh ** -0.5
    outs = []
    for h in range(N_XHEADS):
        sl = slice(h * dh, (h + 1) * dh)
        s = _dot_nt(q[:, sl], mk_scr[:, sl]) * scale
        s = s - jnp.max(s, axis=-1, keepdims=True)
        e = jnp.exp(s)
        pr = e / jnp.sum(e, axis=-1, keepdims=True)
        outs.append(_dot(pr, mv_scr[:, sl]))
    o = jnp.concatenate(outs, axis=1)
    o_ref[0] = x + _rms(_dot(o, wo_ref[...]), gp_ref[...])


def xattn(x, mk, mv, lpk, tq):
    bsz, t, d = x.shape
    nm = mk.shape[1]
    tq = _row_tile(t, tq)
    params = [lpk[n] for n in ("g_x_pre", "w_q", "w_o_x", "g_x_post")]
    xs = pl.BlockSpec((1, tq, d), lambda bi, ti: (bi, ti, 0))
    ms = pl.BlockSpec((1, nm, d), lambda bi, ti: (bi, 0, 0))
    return pl.pallas_call(
        functools.partial(_xattn_kernel, dh=d // N_XHEADS),
        grid=(bsz, t // tq),
        in_specs=[xs, ms, ms] + [_full(a.shape) for a in params],
        out_specs=xs,
        out_shape=jax.ShapeDtypeStruct(x.shape, F32),
        scratch_shapes=[pltpu.VMEM((nm, d), BF16), pltpu.VMEM((nm, d), BF16)],
        compiler_params=_cparams("parallel", "arbitrary"),
        name="xattn",
    )(x, mk, mv, *params)


def _xattn_step_kernel(q_ref, k_ref, v_ref, o_ref, *, dh):
    q = q_ref[0]
    s = jnp.sum(k_ref[0] * q[None, :, :], axis=-1, keepdims=True) * dh ** -0.5
    s = s - jnp.max(s, axis=0, keepdims=True)
    e = jnp.exp(s)
    pr = e / jnp.sum(e, axis=0, keepdims=True)
    o_ref[0] = jnp.sum(pr * v_ref[0], axis=0)


def xattn_step(q, cache_k, cache_v, layer):
    bsz, d = q.shape
    _, _, nm, nh, dh = cache_k.shape
    qs = pl.BlockSpec((1, nh, dh), lambda i: (i, 0, 0))
    cs = pl.BlockSpec((None, 1, nm, nh, dh), lambda i: (layer, i, 0, 0, 0))
    o = pl.pallas_call(
        functools.partial(_xattn_step_kernel, dh=dh),
        grid=(bsz,),
        in_specs=[qs, cs, cs],
        out_specs=qs,
        out_shape=jax.ShapeDtypeStruct((bsz, nh, dh), F32),
        compiler_params=_cparams("parallel"),
        name="xattn_step",
    )(q.reshape(bsz, nh, dh), cache_k, cache_v)
    return o.reshape(bsz, d)


def _mm_norm_res_kernel(a_ref, x_ref, w_ref, g_ref, o_ref):
    o_ref[...] = x_ref[...] + _rms(_dot(a_ref[...], w_ref[...]), g_ref[...])


def mm_norm_res(a, x, w, g, tm):
    m, k = a.shape
    n = w.shape[1]
    tm = _row_tile(m, tm)
    return pl.pallas_call(
        _mm_norm_res_kernel,
        grid=(m // tm,),
        in_specs=[pl.BlockSpec((tm, k), lambda i: (i, 0)), pl.BlockSpec((tm, n), lambda i: (i, 0)),
                  _full((k, n)), _full((1, n))],
        out_specs=pl.BlockSpec((tm, n), lambda i: (i, 0)),
        out_shape=jax.ShapeDtypeStruct((m, n), F32),
        compiler_params=_cparams("parallel"),
        name="mm_norm_res",
    )(a, x, w, g)


MXU_K = 256


def _ffn_chunks(dff):
    half = -(-(dff // 2) // MXU_K) * MXU_K
    return ((0, half), (half, dff)) if half < dff else ((0, dff),)


def _ffn_kernel(x_ref, gpre_ref, wgu_ref, wd_ref, gpost_ref, o_ref, *, dff):
    x = x_ref[...]
    h = _rms(x, gpre_ref[...]).astype(BF16)
    acc = None
    for lo, hi in _ffn_chunks(dff):
        gt = jnp.dot(h, wgu_ref[:, lo:hi], preferred_element_type=F32)
        up = jnp.dot(h, wgu_ref[:, dff + lo:dff + hi], preferred_element_type=F32)
        part = _dot(gt * _sigmoid(gt) * up, wd_ref[lo:hi, :])
        acc = part if acc is None else acc + part
    o_ref[...] = x + _rms(acc, gpost_ref[...])


def _resident(shape):
    nd = len(shape)
    return pl.BlockSpec(shape, lambda *_: (0,) * nd, pipeline_mode=pl.Buffered(1))


def ffn(x, lpk, tm):
    m, d = x.shape
    w_gu, w_dn = lpk["w_gate_up"], lpk["w_down"]
    dff = w_dn.shape[0]
    assert dff % LANES == 0
    tm = _row_tile(m, tm)
    return pl.pallas_call(
        functools.partial(_ffn_kernel, dff=dff),
        grid=(m // tm,),
        in_specs=[pl.BlockSpec((tm, d), lambda i: (i, 0)), _full((1, d)), _resident(w_gu.shape),
                  _resident(w_dn.shape), _full((1, d))],
        out_specs=pl.BlockSpec((tm, d), lambda i: (i, 0)),
        out_shape=jax.ShapeDtypeStruct((m, d), F32),
        compiler_params=_cparams("parallel"),
        name="ffn",
    )(x, lpk["g_ffn_pre"], w_gu, w_dn, lpk["g_ffn_post"])


def _layer_params(l, w):
    row = lambda a: a[l].reshape(1, -1)
    c = w["w_out_rwkv"].shape[1]
    lora_rows = D_DECAY_LORA + D_ICLR_LORA
    w2p = jnp.zeros((lora_rows, c), F32).at[:D_DECAY_LORA].set(w["decay_w2"][l])
    a2p = jnp.zeros((lora_rows, c), F32).at[D_DECAY_LORA:].set(w["iclr_a2"][l])
    head = jnp.arange(c) // HEAD_SIZE
    lpk = {
        "c_rwkv": c,
        "bd": (head[:, None] == head[None, :]).astype(BF16),
        "decay_w2p": w2p.astype(BF16), "iclr_a2p": a2p.astype(BF16),
        "conv_w": jnp.pad(w["conv_w"][l], ((0, CONV_PAD - CONV_WIDTH), (0, 0))),
    }
    for n in ("w_in", "gate_g2", "w_out_rwkv", "w_pw", "w_out", "w_q", "w_o_x", "w_gate_up", "w_down"):
        lpk[n] = w[n][l].astype(BF16)
    lpk["w_kv"] = jnp.concatenate([w["w_k"][l], w["w_v"][l]], axis=1).astype(BF16)
    for n in ("tshift_mu", "decay_w0", "iclr_a0", "k_k", "k_a", "r_k", "lnx_g", "lnx_b", "glu_b", "conv_b",
              "conv_ln_g", "conv_ln_b", "b_pw", "merge_gate_b", "g_mix_pre", "g_mix_post", "g_mem", "g_x_pre",
              "g_x_post", "g_ffn_pre", "g_ffn_post"):
        lpk[n] = row(w[n])
    return lpk


def _decoder_layer(x, mem_k, mem_v, s0, shift0, buf0, lpk, tm, mem_layer=0):
    bsz, t, d = x.shape
    m = bsz * t
    c = lpk["c_rwkv"]
    x2 = x.reshape(m, d)
    (r, lw, k, v, a, b, g, glu, gate), shift1 = proj_in(x2, shift0, t, lpk, tm)
    if s0 is None:
        seq = lambda z: z.reshape(bsz, t, c)
        y, s1 = wkv_chunked(seq(r), seq(lw), seq(k), seq(v), seq(a), seq(b), nb=2 if bsz % 2 == 0 else 1)
        y = y.reshape(m, c)
        cv, buf1 = conv_seq(seq(glu), buf0, lpk, 256)
        cv = cv.reshape(m, c)
    else:
        y, s1 = wkv_step(s0, r, lw, k, v, a, b)
        cv, buf1 = conv_step(glu, buf0, lpk)
    x2 = merge(x2, y, r, k, v, g, cv, gate, lpk, tm)
    if s0 is not None:
        (q,) = norm_mm(x2, lpk["g_x_pre"], lpk["w_q"], (d,), tm)
        o = xattn_step(q, mem_k, mem_v, mem_layer)
        x2 = mm_norm_res(o, x2, lpk["w_o_x"], lpk["g_x_post"], tm)
    else:
        x2 = xattn(x2.reshape(bsz, t, d), mem_k, mem_v, lpk, 512 if t % 512 == 0 else 256).reshape(m, d)
    x2 = ffn(x2, lpk, 2 * tm)
    return x2.reshape(bsz, t, d), s1, shift1, buf1


def kernel(x_prompt, x_sample, mem_prompt, state_rwkv, state_shift, state_conv, cache_mem_k, cache_mem_v, w_in, tshift_mu, decay_w0, decay_w2, iclr_a0, iclr_a2, gate_g2, k_k, k_a, r_k, lnx_g, lnx_b, w_out_rwkv, glu_b, conv_w, conv_b, conv_ln_g, conv_ln_b, w_pw, b_pw, merge_gate_b, w_out, g_mix_pre, g_mix_post, g_mem, w_q, w_k, w_v, w_o_x, g_x_pre, g_x_post, g_ffn_pre, g_ffn_post, w_gate_up, w_down):
    w = dict(w_in=w_in, tshift_mu=tshift_mu, decay_w0=decay_w0, decay_w2=decay_w2, iclr_a0=iclr_a0,
             iclr_a2=iclr_a2, gate_g2=gate_g2, k_k=k_k, k_a=k_a, r_k=r_k, lnx_g=lnx_g, lnx_b=lnx_b,
             w_out_rwkv=w_out_rwkv, glu_b=glu_b, conv_w=conv_w, conv_b=conv_b, conv_ln_g=conv_ln_g,
             conv_ln_b=conv_ln_b, w_pw=w_pw, b_pw=b_pw, merge_gate_b=merge_gate_b, w_out=w_out,
             g_mix_pre=g_mix_pre, g_mix_post=g_mix_post, g_mem=g_mem, w_q=w_q, w_k=w_k, w_v=w_v, w_o_x=w_o_x,
             g_x_pre=g_x_pre, g_x_post=g_x_post, g_ffn_pre=g_ffn_pre, g_ffn_post=g_ffn_post,
             w_gate_up=w_gate_up, w_down=w_down)
    depth = w_in.shape[0]
    bp, _, d = x_prompt.shape
    bs = x_sample.shape[0]
    nm = mem_prompt.shape[1]
    c = w_out_rwkv.shape[1]
    wr = state_shift.shape[-1]
    xp, xs = x_prompt, x_sample
    mem2 = mem_prompt.reshape(bp * nm, d)
    outs = {n: [] for n in ("rwkv_p", "shift_p", "conv_p", "memk_p", "memv_p", "rwkv_s", "shift_s", "conv_s")}
    for l in range(depth):
        lpk = _layer_params(l, w)
        mk, mv = norm_mm(mem2, lpk["g_mem"], lpk["w_kv"], (d, d), 256)
        mk, mv = mk.reshape(bp, nm, d), mv.reshape(bp, nm, d)
        xp, s_p, sh_p, cb_p = _decoder_layer(xp, mk, mv, None, jnp.zeros((bp, wr), F32),
                                             jnp.zeros((bp, CONV_WIDTH - 1, c), F32), lpk, 256)
        xs, s_s, sh_s, cb_s = _decoder_layer(xs, cache_mem_k, cache_mem_v, state_rwkv[l], state_shift[l],
                                             state_conv[l], lpk, 128, mem_layer=l)
        dh = d // N_XHEADS
        outs["rwkv_p"].append(s_p); outs["shift_p"].append(sh_p); outs["conv_p"].append(cb_p)
        outs["memk_p"].append(mk.reshape(bp, nm, N_XHEADS, dh)); outs["memv_p"].append(mv.reshape(bp, nm, N_XHEADS, dh))
        outs["rwkv_s"].append(s_s); outs["shift_s"].append(sh_s); outs["conv_s"].append(cb_s)
    st = {n: jnp.stack(v) for n, v in outs.items()}
    return (xp, xs, st["rwkv_p"], st["shift_p"], st["conv_p"], st["memk_p"], st["memv_p"],
            st["rwkv_s"], st["shift_s"], st["conv_s"])
```

```python
import functools
import math

import jax
import jax.numpy as jnp
from jax import lax
from jax.experimental import pallas as pl
from jax.experimental.pallas import tpu as pltpu

F32 = jnp.float32
BF16 = jnp.bfloat16

HEAD_SIZE = 64
D_DECAY_LORA = 64
D_ICLR_LORA = 64
D_GATE_LORA = 128
CONV_WIDTH = 31
N_XHEADS = 4
GN_EPS = 64e-5
NORM_EPS = 1e-6
LN_EPS = 1e-5

LANES = 128
SUBLANES = 8
VMEM_LIMIT_BYTES = 56 * 1024 * 1024

WKV_CHUNK = 64


def _cparams(*sem):
    return pltpu.CompilerParams(dimension_semantics=sem, vmem_limit_bytes=VMEM_LIMIT_BYTES)


def _dot(a, b):
    return jnp.dot(a.astype(BF16), b.astype(BF16), preferred_element_type=F32)


def _dot_nt(a, b):
    return lax.dot_general(a.astype(BF16), b.astype(BF16), (((1,), (1,)), ((), ())),
                           preferred_element_type=F32)


def _split3(x):
    hi = x.astype(BF16)
    r1 = x - hi.astype(F32)
    mid = r1.astype(BF16)
    lo = (r1 - mid.astype(F32)).astype(BF16)
    return hi, mid, lo


def _segsum(x, bd):
    return jnp.dot(x.astype(BF16), bd, preferred_element_type=F32)


def _rms(x, g):
    return x * lax.rsqrt(jnp.mean(x * x, axis=-1, keepdims=True) + NORM_EPS) * g


def _sigmoid(x):
    return 0.5 * jnp.tanh(0.5 * x) + 0.5


def _row_tile(m, target):
    t = min(m, target)
    assert m % t == 0, (m, t)
    return t


def _full(shape):
    nd = len(shape)
    return pl.BlockSpec(shape, lambda *_: (0,) * nd)


def _norm_mm_kernel(x_ref, g_ref, w_ref, *o_refs, splits):
    h = _rms(x_ref[...], g_ref[...]).astype(BF16)
    off = 0
    for o_ref, n in zip(o_refs, splits):
        o_ref[...] = jnp.dot(h, w_ref[:, off:off + n], preferred_element_type=F32)
        off += n


def norm_mm(x, g, w, splits, tm):
    m, k = x.shape
    tm = _row_tile(m, tm)
    n_all = w.shape[1]
    assert sum(splits) == n_all and all(s % LANES == 0 for s in splits)
    return pl.pallas_call(
        functools.partial(_norm_mm_kernel, splits=tuple(splits)),
        grid=(m // tm,),
        in_specs=[pl.BlockSpec((tm, k), lambda i: (i, 0)), _full((1, k)), _full((k, n_all))],
        out_specs=[pl.BlockSpec((tm, n), lambda i: (i, 0)) for n in splits],
        out_shape=[jax.ShapeDtypeStruct((m, n), F32) for n in splits],
        compiler_params=_cparams("parallel"),
        name="norm_mm",
    )(x, g.reshape(1, k), w)


def _proj_in_kernel(x_ref, sh_ref, gpre_ref, w_ref, mu_ref, w0_ref, w2_ref, a0_ref, a2_ref, g2_ref, kk_ref,
                    ka_ref, bd_ref, gb_ref, mb_ref, r_ref, lw_ref, k_ref, v_ref, a_ref, b_ref, g_ref, glu_ref,
                    gate_ref, last_ref, prev_scr, *, c, tiles_per_seq):
    h = _rms(x_ref[...], gpre_ref[...]).astype(BF16)
    wr = 3 * c + D_DECAY_LORA + D_ICLR_LORA + D_GATE_LORA
    p = jnp.dot(h, w_ref[:, :wr], preferred_element_type=F32)
    u = jnp.dot(h, w_ref[:, wr:wr + 2 * c], preferred_element_type=F32) + gb_ref[...]
    glu_ref[...] = u[:, :c] * _sigmoid(u[:, c:])
    gate_ref[...] = _sigmoid(jnp.dot(h, w_ref[:, wr + 2 * c:], preferred_element_type=F32) + mb_ref[...])
    if tiles_per_seq == 0:
        pp = sh_ref[...]
        last_ref[...] = p
    else:
        tm = p.shape[0]

        @pl.when(pl.program_id(0) == 0)
        def _():
            prev_scr[...] = jnp.zeros_like(prev_scr)

        first = (pl.program_id(0) % tiles_per_seq) == 0
        prev_row = jnp.where(first, sh_ref[0], prev_scr[...])
        rows = lax.broadcasted_iota(jnp.int32, p.shape, 0)
        pp = jnp.where(rows == 0, prev_row, pltpu.roll(p, 1, axis=0))
        prev_scr[...] = p[tm - 1:tm, :]
        last_ref[0] = p[tm - 1:tm, :]
    xs = p + (pp - p) * mu_ref[...]
    r = xs[:, 0:c]
    k = xs[:, c:2 * c]
    v = xs[:, 2 * c:3 * c]
    wa = xs[:, 3 * c:3 * c + D_DECAY_LORA + D_ICLR_LORA]
    gd = xs[:, 3 * c + D_DECAY_LORA + D_ICLR_LORA:]
    z = w0_ref[...] + _dot(jnp.tanh(wa), w2_ref[...])
    lw_ref[...] = -math.exp(-0.5) * _sigmoid(z)
    a = _sigmoid(a0_ref[...] + _dot(wa, a2_ref[...]))
    g_ref[...] = _dot(_sigmoid(gd), g2_ref[...])
    kk = k * kk_ref[...]
    ss = _segsum(kk * kk, bd_ref[...])
    kk = kk * lax.rsqrt(jnp.maximum(ss, 1e-24))
    r_ref[...] = r
    k_ref[...] = k * (1.0 + (a - 1.0) * ka_ref[...])
    v_ref[...] = v
    a_ref[...] = -kk
    b_ref[...] = kk * a


def proj_in(x, shift0, seq_len, lpk, tm):
    m, d = x.shape
    c = lpk["c_rwkv"]
    wr = shift0.shape[-1]
    nseq = m // seq_len
    tm = _row_tile(m, tm)
    row = lambda n: pl.BlockSpec((tm, n), lambda i: (i, 0))
    if seq_len == 1:
        tiles_per_seq, sh, sh_spec = 0, shift0, row(wr)
        last_spec, last_shape = row(wr), (m, wr)
    else:
        assert seq_len % tm == 0
        tiles_per_seq = seq_len // tm
        sh = shift0.reshape(nseq, 1, wr)
        sh_spec = pl.BlockSpec((1, 1, wr), lambda i: (i // tiles_per_seq, 0, 0))
        last_spec, last_shape = sh_spec, (nseq, 1, wr)
    names = ("g_mix_pre", "w_in", "tshift_mu", "decay_w0", "decay_w2p", "iclr_a0", "iclr_a2p", "gate_g2", "k_k",
             "k_a", "bd", "glu_b", "merge_gate_b")
    params = [lpk[n] for n in names]
    outs = pl.pallas_call(
        functools.partial(_proj_in_kernel, c=c, tiles_per_seq=tiles_per_seq),
        grid=(m // tm,),
        in_specs=[row(d), sh_spec] + [_full(a.shape) for a in params],
        out_specs=[row(c)] * 8 + [row(2 * d), last_spec],
        out_shape=[jax.ShapeDtypeStruct((m, c), F32)] * 8 + [jax.ShapeDtypeStruct((m, 2 * d), F32),
                                                             jax.ShapeDtypeStruct(last_shape, F32)],
        scratch_shapes=[pltpu.VMEM((1, wr), F32)],
        compiler_params=_cparams("arbitrary"),
        name="proj_in",
    )(x, sh, *params)
    return outs[:9], outs[9].reshape(nseq, wr)


def _wkv_chunk_kernel(r_ref, lw_ref, k_ref, v_ref, a_ref, b_ref, y_ref, s_ref, g_scr, *, L, npairs, nb):
    ci = pl.program_id(1)

    @pl.when(ci == 0)
    def _():
        g_scr[...] = jnp.zeros_like(g_scr)

    row = lax.broadcasted_iota(jnp.int32, (L, L), 0)
    col = lax.broadcasted_iota(jnp.int32, (L, L), 1)
    ltri = jnp.where(col <= row, 1.0, 0.0).astype(BF16)
    d = functools.partial(jnp.dot, preferred_element_type=F32)
    m0 = lax.broadcasted_iota(jnp.int32, (L, LANES), 1) < HEAD_SIZE
    i2 = lax.broadcasted_iota(jnp.int32, (2 * L, 2 * L), 0)
    j2 = lax.broadcasted_iota(jnp.int32, (2 * L, 2 * L), 1)
    same = (i2 < L) == (j2 < L)
    tril_s = same & (j2 < i2)
    tril_i = same & (j2 <= i2)
    eye = jnp.where(i2 == j2, 1.0, 0.0)

    def stack(x, p):
        xp = x[:, p * LANES:(p + 1) * LANES]
        return jnp.concatenate([jnp.where(m0, xp, 0.0), jnp.where(m0, 0.0, xp)], axis=0)

    inst = [(j, p) for j in range(nb) for p in range(npairs)]
    lhs, rhs, vm, p_last = [], [], [], []
    for j in range(nb):
        lw = lw_ref[j]
        hi, mid, lo = _split3(lw)
        cs = d(ltri, hi) + (d(ltri, mid) + d(ltri, lo))
        pin = jnp.exp(cs)
        pinv = jnp.exp(-cs)
        at = a_ref[j] * jnp.exp(cs - lw)
        rt = r_ref[j] * pin
        bt = b_ref[j] * pinv
        kt = k_ref[j] * pinv
        v = v_ref[j]
        for p in range(npairs):
            lhs.append(jnp.concatenate([stack(at, p), stack(rt, p)], axis=0).astype(BF16))
            rhs.append(jnp.concatenate([stack(bt, p), stack(kt, p)], axis=0).astype(BF16))
            vm.append(stack(v, p).astype(BF16))
            p_last.append(pin[L - 1:L, p * LANES:(p + 1) * LANES])
    n = len(inst)
    gm = [_dot_nt(lhs[i], rhs[i]) for i in range(n)]
    a_ab = [jnp.where(tril_s, gm[i][:2 * L, :2 * L], 0.0) for i in range(n)]
    a_kr = [jnp.concatenate([jnp.where(tril_s, gm[i][:2 * L, 2 * L:], 0.0),
                             jnp.where(tril_i, gm[i][2 * L:, 2 * L:], 0.0)], axis=0) for i in range(n)]
    a_rb = [jnp.where(tril_i, gm[i][2 * L:, :2 * L], 0.0) for i in range(n)]
    mv = [_dot(a_kr[i], vm[i]) for i in range(n)]
    g = [g_scr[j, p] for (j, p) in inst]
    gh = [_dot_nt(lhs[i], g[i]) for i in range(n)]
    t = [eye + a_ab[i] for i in range(n)]
    ak = [_dot(a_ab[i], a_ab[i]) for i in range(n)]
    for _ in range(int(math.log2(L)) - 2):
        pr = [_dot(ak[i], jnp.concatenate([ak[i], t[i]], axis=1)) for i in range(n)]
        ak = [pr[i][:, :2 * L] for i in range(n)]
        t = [t[i] + pr[i][:, 2 * L:] for i in range(n)]
    t = [t[i] + _dot(ak[i], t[i]) for i in range(n)]
    tc = [jnp.concatenate([t[i], _dot(a_rb[i], t[i])], axis=0) for i in range(n)]
    uy = [_dot(tc[i], gh[i][:2 * L] + mv[i][:2 * L]) for i in range(n)]
    for i, (j, p) in enumerate(inst):
        yy = gh[i][2 * L:] + mv[i][2 * L:] + uy[i][2 * L:]
        y_ref[j, :, p * LANES:(p + 1) * LANES] = yy[:L] + yy[L:]
        uv = jnp.concatenate([uy[i][:2 * L], vm[i].astype(F32)], axis=0)
        upd = _dot(uv.T, rhs[i])
        g_scr[j, p] = (g[i] + upd) * p_last[i]

    @pl.when(ci == pl.num_programs(1) - 1)
    def _():
        s_ref[...] = g_scr[...]


def wkv_chunked(r, lw, k, v, a, b, nb=1):
    bsz, t, c = r.shape
    L = WKV_CHUNK
    assert t % L == 0 and c % LANES == 0 and bsz % nb == 0
    npairs = c // LANES
    seq = pl.BlockSpec((nb, L, c), lambda bi, ci: (bi, ci, 0))
    y, s = pl.pallas_call(
        functools.partial(_wkv_chunk_kernel, L=L, npairs=npairs, nb=nb),
        grid=(bsz // nb, t // L),
        in_specs=[seq] * 6,
        out_specs=[seq, pl.BlockSpec((nb, npairs, LANES, LANES), lambda bi, ci: (bi, 0, 0, 0))],
        out_shape=[jax.ShapeDtypeStruct((bsz, t, c), F32),
                   jax.ShapeDtypeStruct((bsz, npairs, LANES, LANES), F32)],
        scratch_shapes=[pltpu.VMEM((nb, npairs, LANES, LANES), F32)],
        compiler_params=_cparams("parallel", "arbitrary"),
        name="wkv_chunked",
    )(r, lw, k, v, a, b)
    n = HEAD_SIZE
    s = jnp.stack([s[:, :, :n, :n], s[:, :, n:, n:]], axis=2)
    return y, s.reshape(bsz, 2 * npairs, n, n)


def _wkv_step_kernel(s_ref, r_ref, lw_ref, k_ref, v_ref, a_ref, b_ref, so_ref, y_ref, *, bb, nh):
    n = HEAD_SIZE
    eye = jnp.where(lax.broadcasted_iota(jnp.int32, (n, n), 0) == lax.broadcasted_iota(jnp.int32, (n, n), 1),
                    1.0, 0.0)
    pad = jnp.zeros((SUBLANES - 3, n), F32)

    def body(i, carry):
        r, k, v, a, b = r_ref[i], k_ref[i], v_ref[i], a_ref[i], b_ref[i]
        w = jnp.exp(lw_ref[i])
        wr = w * r
        br = jnp.sum(b * r, axis=1, keepdims=True)
        kr = jnp.sum(k * r, axis=1, keepdims=True)
        hs = [slice(h, h + 1) for h in range(nh)]
        s = [s_ref[i, h] for h in range(nh)]
        x8 = [jnp.concatenate([a[hs[h]], wr[hs[h]], v[hs[h]], pad], axis=0) for h in range(nh)]
        rows = [_dot_nt(x8[h], s[h]) for h in range(nh)]
        se = [jnp.concatenate([s[h], eye], axis=0).astype(BF16) for h in range(nh)]
        x8s = [_split3(x8[h]) for h in range(nh)]
        cols = [_dot_nt(se[h], x8s[h][0]) + (_dot_nt(se[h], x8s[h][1]) + _dot_nt(se[h], x8s[h][2]))
                for h in range(nh)]
        ys = []
        for h in range(nh):
            sa_col = cols[h][:n, 0:1]
            v_col = cols[h][n:, 2:3]
            so_ref[i, h] = s[h] * w[hs[h]] + sa_col * b[hs[h]] + v_col * k[hs[h]]
            ys.append(rows[h][1:2] + rows[h][0:1] * br[hs[h]] + v[hs[h]] * kr[hs[h]])
        y_ref[i] = jnp.concatenate(ys, axis=0)
        return carry

    lax.fori_loop(0, bb, body, 0)


def wkv_step(s_all, layer, r, lw, k, v, a, b):
    _, bsz, nh, n, _ = s_all.shape
    bb = _row_tile(bsz, 8)
    vec = pl.BlockSpec((bb, nh, n), lambda i: (i, 0, 0))
    st = pl.BlockSpec((None, bb, nh, n, n), lambda i: (layer, i, 0, 0, 0))
    hv = lambda x: x.reshape(bsz, nh, n)
    s_all, y = pl.pallas_call(
        functools.partial(_wkv_step_kernel, bb=bb, nh=nh),
        grid=(bsz // bb,),
        in_specs=[st] + [vec] * 6,
        out_specs=[st, vec],
        out_shape=[jax.ShapeDtypeStruct(s_all.shape, F32), jax.ShapeDtypeStruct((bsz, nh, n), F32)],
        input_output_aliases={0: 0},
        compiler_params=_cparams("parallel"),
        name="wkv_step",
    )(s_all, hv(r), hv(lw), hv(k), hv(v), hv(a), hv(b))
    return y.reshape(bsz, nh * n), s_all


CONV_PAD = 32


CONV_ROWS = 32


def _ln_swish(y, g_ref, b_ref):
    mu = jnp.mean(y, axis=-1, keepdims=True)
    dlt = y - mu
    var = jnp.mean(dlt * dlt, axis=-1, keepdims=True)
    y = dlt * lax.rsqrt(var + LN_EPS) * g_ref[...] + b_ref[...]
    return y * _sigmoid(y)


def _conv_seq_kernel(u_ref, buf_ref, cw_ref, cb_ref, lg_ref, lb_ref, o_ref, nb_ref, xp_scr, *, tt):
    ti = pl.program_id(1)

    @pl.when(ti == 0)
    def _():
        xp_scr[0:CONV_PAD, :] = buf_ref[0]

    xp_scr[CONV_PAD:CONV_PAD + tt, :] = u_ref[0]
    base = CONV_PAD - (CONV_WIDTH - 1)
    rb = CONV_ROWS

    for i in range(tt // rb):
        s0 = i * rb
        acc = None
        for r in range(SUBLANES):
            rows = rb if r == 0 else rb + SUBLANES
            part = None
            for j in range((CONV_PAD + SUBLANES) // SUBLANES):
                w = SUBLANES * j + r - base
                if 0 <= w < CONV_WIDTH:
                    term = xp_scr[pl.ds(s0 + SUBLANES * j, rows), :] * cw_ref[w:w + 1, :]
                    part = term if part is None else part + term
            part = part if r == 0 else part[r:r + rb]
            acc = part if acc is None else acc + part
        o_ref[0, pl.ds(s0, rb), :] = _ln_swish(acc + cb_ref[...], lg_ref, lb_ref)
    tail = xp_scr[tt:tt + CONV_PAD, :]
    xp_scr[0:CONV_PAD, :] = tail

    @pl.when(ti == pl.num_programs(1) - 1)
    def _():
        nb_ref[0] = tail


def conv_seq(u, buf0, lpk, tt):
    bsz, t, c = u.shape
    tt = _row_tile(t, tt)
    assert tt % CONV_ROWS == 0
    pad = CONV_PAD - (CONV_WIDTH - 1)
    bufp = jnp.pad(buf0, ((0, 0), (pad, 0), (0, 0)))
    params = [lpk[n] for n in ("conv_w", "conv_b", "conv_ln_g", "conv_ln_b")]
    hist = pl.BlockSpec((1, CONV_PAD, c), lambda bi, ti: (bi, 0, 0))
    out, nb = pl.pallas_call(
        functools.partial(_conv_seq_kernel, tt=tt),
        grid=(bsz, t // tt),
        in_specs=[pl.BlockSpec((1, tt, c), lambda bi, ti: (bi, ti, 0)), hist] + [_full(a.shape) for a in params],
        out_specs=[pl.BlockSpec((1, tt, c), lambda bi, ti: (bi, ti, 0)), hist],
        out_shape=[jax.ShapeDtypeStruct((bsz, t, c), F32), jax.ShapeDtypeStruct((bsz, CONV_PAD, c), F32)],
        scratch_shapes=[pltpu.VMEM((CONV_PAD + tt, c), F32)],
        compiler_params=_cparams("parallel", "arbitrary"),
        name="conv_seq",
    )(u, bufp, *params)
    return out, nb[:, pad:]


def _conv_step_kernel(u_ref, buf_ref, cw_ref, cb_ref, lg_ref, lb_ref, o_ref, nb_ref, *, bb):
    hist = CONV_WIDTH - 1
    glu = u_ref[...]
    w_hist = cw_ref[0:hist, :]
    w_last = cw_ref[hist:hist + 1, :]
    rows = []
    for i in range(bb):
        rows.append(jnp.sum(buf_ref[i] * w_hist, axis=0, keepdims=True))
    acc = jnp.concatenate(rows, axis=0) + glu * w_last + cb_ref[...]
    o_ref[...] = _ln_swish(acc, lg_ref, lb_ref)
    for i in range(bb):
        nb_ref[i, 0:hist - 1, :] = buf_ref[i, 1:hist, :]
        nb_ref[i, hist - 1:hist, :] = glu[i:i + 1, :]


def conv_step(u, buf_all, layer, lpk):
    bsz, c = u.shape
    bb = _row_tile(bsz, 8)
    params = [lpk[n] for n in ("conv_w", "conv_b", "conv_ln_g", "conv_ln_b")]
    hist = pl.BlockSpec((None, bb, CONV_WIDTH - 1, c), lambda i: (layer, i, 0, 0))
    return pl.pallas_call(
        functools.partial(_conv_step_kernel, bb=bb),
        grid=(bsz // bb,),
        in_specs=[pl.BlockSpec((bb, c), lambda i: (i, 0)), hist] + [_full(a.shape) for a in params],
        out_specs=[pl.BlockSpec((bb, c), lambda i: (i, 0)), hist],
        out_shape=[jax.ShapeDtypeStruct((bsz, c), F32), jax.ShapeDtypeStruct(buf_all.shape, F32)],
        input_output_aliases={1: 1},
        compiler_params=_cparams("parallel"),
        name="conv_step",
    )(u, buf_all, *params)


def _merge_kernel(x_ref, y_ref, r_ref, k_ref, v_ref, g_ref, cv_ref, gate_ref, bd_ref, lg_ref, lb_ref, rk_ref,
                  wa_ref, wp_ref, bp_ref, wo_ref, gp_ref, o_ref, *, d):
    bd = bd_ref[...]
    inv_n = 1.0 / HEAD_SIZE
    y = y_ref[...]
    mu = _segsum(y, bd) * inv_n
    dlt = y - mu
    var = _segsum(dlt * dlt, bd) * inv_n
    yn = dlt * lax.rsqrt(var + GN_EPS) * lg_ref[...] + lb_ref[...]
    v = v_ref[...]
    bonus = _segsum(r_ref[...] * k_ref[...] * rk_ref[...], bd) * v
    za = (yn + bonus) * g_ref[...]
    ya = _dot(za, wa_ref[...])
    yb = _dot(cv_ref[...], wp_ref[...]) + bp_ref[...]
    mixed = _dot(gate_ref[:, :d] * ya + gate_ref[:, d:] * yb, wo_ref[...])
    o_ref[...] = x_ref[...] + _rms(mixed, gp_ref[...])


def merge(x, y, r, k, v, g, cv, gate, lpk, tm):
    m, d = x.shape
    c = y.shape[1]
    tm = _row_tile(m, tm)
    row = lambda n: pl.BlockSpec((tm, n), lambda i: (i, 0))
    names = ("bd", "lnx_g", "lnx_b", "r_k", "w_out_rwkv", "w_pw", "b_pw", "w_out", "g_mix_post")
    params = [lpk[n] for n in names]
    return pl.pallas_call(
        functools.partial(_merge_kernel, d=d),
        grid=(m // tm,),
        in_specs=[row(d)] + [row(c)] * 6 + [row(2 * d)] + [_full(a.shape) for a in params],
        out_specs=row(d),
        out_shape=jax.ShapeDtypeStruct((m, d), F32),
        compiler_params=_cparams("parallel"),
        name="merge",
    )(x, y, r, k, v, g, cv, gate, *params)


def _xattn_kernel(x_ref, mk_ref, mv_ref, gq_ref, wq_ref, wo_ref, gp_ref, o_ref, mk_scr, mv_scr, *, dh):
    @pl.when(pl.program_id(1) == 0)
    def _():
        mk_scr[...] = mk_ref[0].astype(BF16)
        mv_scr[...] = mv_ref[0].astype(BF16)

    x = x_ref[0]
    q = _dot(_rms(x, gq_ref[...]), wq_ref[...])
    scale = dh ** -0.5
    outs = []
    for h in range(N_XHEADS):
        sl = slice(h * dh, (h + 1) * dh)
        s = _dot_nt(q[:, sl], mk_scr[:, sl]) * scale
        s = s - jnp.max(s, axis=-1, keepdims=True)
        e = jnp.exp(s)
        pr = e / jnp.sum(e, axis=-1, keepdims=True)
        outs.append(_dot(pr, mv_scr[:, sl]))
    o = jnp.concatenate(outs, axis=1)
    o_ref[0] = x + _rms(_dot(o, wo_ref[...]), gp_ref[...])


def xattn(x, mk, mv, lpk, tq):
    bsz, t, d = x.shape
    nm = mk.shape[1]
    tq = _row_tile(t, tq)
    params = [lpk[n] for n in ("g_x_pre", "w_q", "w_o_x", "g_x_post")]
    xs = pl.BlockSpec((1, tq, d), lambda bi, ti: (bi, ti, 0))
    ms = pl.BlockSpec((1, nm, d), lambda bi, ti: (bi, 0, 0))
    return pl.pallas_call(
        functools.partial(_xattn_kernel, dh=d // N_XHEADS),
        grid=(bsz, t // tq),
        in_specs=[xs, ms, ms] + [_full(a.shape) for a in params],
        out_specs=xs,
        out_shape=jax.ShapeDtypeStruct(x.shape, F32),
        scratch_shapes=[pltpu.VMEM((nm, d), BF16), pltpu.VMEM((nm, d), BF16)],
        compiler_params=_cparams("parallel", "arbitrary"),
        name="xattn",
    )(x, mk, mv, *params)


def _xattn_step_kernel(q_ref, k_ref, v_ref, o_ref, *, bb, nm, nh, dh):
    assert 2 * nh == SUBLANES
    g = nm * nh // SUBLANES
    for j in range(bb):
        q = q_ref[j]
        q8 = jnp.concatenate([q, q], axis=0)
        kf = k_ref[j].reshape(nm * nh, dh).reshape(g, SUBLANES, dh)
        vf = v_ref[j].reshape(nm * nh, dh).reshape(g, SUBLANES, dh)
        s = jnp.sum(kf * q8[None], axis=-1, keepdims=True) * dh ** -0.5
        mx = jnp.max(s, axis=0, keepdims=True)
        mx = jnp.maximum(mx, pltpu.roll(mx[0], nh, axis=0)[None])
        e = jnp.exp(s - mx)
        den = jnp.sum(e, axis=0, keepdims=True)
        den = den + pltpu.roll(den[0], nh, axis=0)[None]
        o8 = jnp.sum((e / den) * vf, axis=0)
        o_ref[j] = o8[:nh] + o8[nh:]


def xattn_step(q, cache_k, cache_v, layer):
    bsz, d = q.shape
    _, _, nm, nh, dh = cache_k.shape
    bb = _row_tile(bsz, 2)
    qs = pl.BlockSpec((bb, nh, dh), lambda i: (i, 0, 0))
    cs = pl.BlockSpec((None, bb, nm, nh, dh), lambda i: (layer, i, 0, 0, 0))
    o = pl.pallas_call(
        functools.partial(_xattn_step_kernel, bb=bb, nm=nm, nh=nh, dh=dh),
        grid=(bsz // bb,),
        in_specs=[qs, cs, cs],
        out_specs=qs,
        out_shape=jax.ShapeDtypeStruct((bsz, nh, dh), F32),
        compiler_params=_cparams("parallel"),
        name="xattn_step",
    )(q.reshape(bsz, nh, dh), cache_k, cache_v)
    return o.reshape(bsz, d)


def _mm_norm_res_kernel(a_ref, x_ref, w_ref, g_ref, o_ref):
    o_ref[...] = x_ref[...] + _rms(_dot(a_ref[...], w_ref[...]), g_ref[...])


def mm_norm_res(a, x, w, g, tm):
    m, k = a.shape
    n = w.shape[1]
    tm = _row_tile(m, tm)
    return pl.pallas_call(
        _mm_norm_res_kernel,
        grid=(m // tm,),
        in_specs=[pl.BlockSpec((tm, k), lambda i: (i, 0)), pl.BlockSpec((tm, n), lambda i: (i, 0)),
                  _full((k, n)), _full((1, n))],
        out_specs=pl.BlockSpec((tm, n), lambda i: (i, 0)),
        out_shape=jax.ShapeDtypeStruct((m, n), F32),
        compiler_params=_cparams("parallel"),
        name="mm_norm_res",
    )(a, x, w, g)


MXU_K = 256


def _ffn_chunks(dff):
    half = -(-(dff // 2) // MXU_K) * MXU_K
    return ((0, half), (half, dff)) if half < dff else ((0, dff),)


def _ffn_kernel(x_ref, gpre_ref, wgu_ref, wd_ref, gpost_ref, o_ref, *, dff):
    x = x_ref[...]
    h = _rms(x, gpre_ref[...]).astype(BF16)
    acc = None
    for lo, hi in _ffn_chunks(dff):
        gt = jnp.dot(h, wgu_ref[:, lo:hi], preferred_element_type=F32)
        up = jnp.dot(h, wgu_ref[:, dff + lo:dff + hi], preferred_element_type=F32)
        part = _dot(gt * _sigmoid(gt) * up, wd_ref[lo:hi, :])
        acc = part if acc is None else acc + part
    o_ref[...] = x + _rms(acc, gpost_ref[...])


def _resident(shape):
    nd = len(shape)
    return pl.BlockSpec(shape, lambda *_: (0,) * nd, pipeline_mode=pl.Buffered(1))


def ffn(x, lpk, tm):
    m, d = x.shape
    w_gu, w_dn = lpk["w_gate_up"], lpk["w_down"]
    dff = w_dn.shape[0]
    assert dff % LANES == 0
    tm = _row_tile(m, tm)
    return pl.pallas_call(
        functools.partial(_ffn_kernel, dff=dff),
        grid=(m // tm,),
        in_specs=[pl.BlockSpec((tm, d), lambda i: (i, 0)), _full((1, d)), _resident(w_gu.shape),
                  _resident(w_dn.shape), _full((1, d))],
        out_specs=pl.BlockSpec((tm, d), lambda i: (i, 0)),
        out_shape=jax.ShapeDtypeStruct((m, d), F32),
        compiler_params=_cparams("parallel"),
        name="ffn",
    )(x, lpk["g_ffn_pre"], w_gu, w_dn, lpk["g_ffn_post"])


def _layer_params(l, w):
    row = lambda a: a[l].reshape(1, -1)
    c = w["w_out_rwkv"].shape[1]
    lora_rows = D_DECAY_LORA + D_ICLR_LORA
    w2p = jnp.zeros((lora_rows, c), F32).at[:D_DECAY_LORA].set(w["decay_w2"][l])
    a2p = jnp.zeros((lora_rows, c), F32).at[D_DECAY_LORA:].set(w["iclr_a2"][l])
    head = jnp.arange(c) // HEAD_SIZE
    lpk = {
        "c_rwkv": c,
        "bd": (head[:, None] == head[None, :]).astype(BF16),
        "decay_w2p": w2p.astype(BF16), "iclr_a2p": a2p.astype(BF16),
        "conv_w": jnp.pad(w["conv_w"][l], ((0, CONV_PAD - CONV_WIDTH), (0, 0))),
    }
    for n in ("w_in", "gate_g2", "w_out_rwkv", "w_pw", "w_out", "w_q", "w_o_x", "w_gate_up", "w_down"):
        lpk[n] = w[n][l].astype(BF16)
    lpk["w_kv"] = jnp.concatenate([w["w_k"][l], w["w_v"][l]], axis=1).astype(BF16)
    for n in ("tshift_mu", "decay_w0", "iclr_a0", "k_k", "k_a", "r_k", "lnx_g", "lnx_b", "glu_b", "conv_b",
              "conv_ln_g", "conv_ln_b", "b_pw", "merge_gate_b", "g_mix_pre", "g_mix_post", "g_mem", "g_x_pre",
              "g_x_post", "g_ffn_pre", "g_ffn_post"):
        lpk[n] = row(w[n])
    return lpk


PROMPT_TM = 256
SAMPLE_TM = 128


def _prompt_layer(x, mem_k, mem_v, shift0, buf0, lpk):
    bsz, t, d = x.shape
    m = bsz * t
    c = lpk["c_rwkv"]
    x2 = x.reshape(m, d)
    (r, lw, k, v, a, b, g, glu, gate), shift1 = proj_in(x2, shift0, t, lpk, PROMPT_TM)
    seq = lambda z: z.reshape(bsz, t, c)
    y, s1 = wkv_chunked(seq(r), seq(lw), seq(k), seq(v), seq(a), seq(b), nb=2 if bsz % 2 == 0 else 1)
    cv, buf1 = conv_seq(seq(glu), buf0, lpk, 256)
    x2 = merge(x2, y.reshape(m, c), r, k, v, g, cv.reshape(m, c), gate, lpk, PROMPT_TM)
    x2 = xattn(x2.reshape(bsz, t, d), mem_k, mem_v, lpk, 512 if t % 512 == 0 else 256).reshape(m, d)
    x2 = ffn(x2, lpk, 2 * PROMPT_TM)
    return x2.reshape(bsz, t, d), s1, shift1, buf1


def _sample_layer(x2, cache_k, cache_v, s_all, shift0, buf_all, layer, lpk):
    d = x2.shape[1]
    (r, lw, k, v, a, b, g, glu, gate), shift1 = proj_in(x2, shift0, 1, lpk, SAMPLE_TM)
    y, s_all = wkv_step(s_all, layer, r, lw, k, v, a, b)
    cv, buf_all = conv_step(glu, buf_all, layer, lpk)
    x2 = merge(x2, y, r, k, v, g, cv, gate, lpk, SAMPLE_TM)
    (q,) = norm_mm(x2, lpk["g_x_pre"], lpk["w_q"], (d,), SAMPLE_TM)
    o = xattn_step(q, cache_k, cache_v, layer)
    x2 = mm_norm_res(o, x2, lpk["w_o_x"], lpk["g_x_post"], SAMPLE_TM)
    x2 = ffn(x2, lpk, SAMPLE_TM)
    return x2, s_all, shift1, buf_all


def kernel(x_prompt, x_sample, mem_prompt, state_rwkv, state_shift, state_conv, cache_mem_k, cache_mem_v, w_in, tshift_mu, decay_w0, decay_w2, iclr_a0, iclr_a2, gate_g2, k_k, k_a, r_k, lnx_g, lnx_b, w_out_rwkv, glu_b, conv_w, conv_b, conv_ln_g, conv_ln_b, w_pw, b_pw, merge_gate_b, w_out, g_mix_pre, g_mix_post, g_mem, w_q, w_k, w_v, w_o_x, g_x_pre, g_x_post, g_ffn_pre, g_ffn_post, w_gate_up, w_down):
    w = dict(w_in=w_in, tshift_mu=tshift_mu, decay_w0=decay_w0, decay_w2=decay_w2, iclr_a0=iclr_a0,
             iclr_a2=iclr_a2, gate_g2=gate_g2, k_k=k_k, k_a=k_a, r_k=r_k, lnx_g=lnx_g, lnx_b=lnx_b,
             w_out_rwkv=w_out_rwkv, glu_b=glu_b, conv_w=conv_w, conv_b=conv_b, conv_ln_g=conv_ln_g,
             conv_ln_b=conv_ln_b, w_pw=w_pw, b_pw=b_pw, merge_gate_b=merge_gate_b, w_out=w_out,
             g_mix_pre=g_mix_pre, g_mix_post=g_mix_post, g_mem=g_mem, w_q=w_q, w_k=w_k, w_v=w_v, w_o_x=w_o_x,
             g_x_pre=g_x_pre, g_x_post=g_x_post, g_ffn_pre=g_ffn_pre, g_ffn_post=g_ffn_post,
             w_gate_up=w_gate_up, w_down=w_down)
    depth = w_in.shape[0]
    bp, _, d = x_prompt.shape
    bs = x_sample.shape[0]
    nm = mem_prompt.shape[1]
    c = w_out_rwkv.shape[1]
    wr = state_shift.shape[-1]
    xp, xs = x_prompt, x_sample.reshape(bs, d)
    mem2 = mem_prompt.reshape(bp * nm, d)
    s_all, buf_all = state_rwkv, state_conv
    outs = {n: [] for n in ("rwkv_p", "shift_p", "conv_p", "memk_p", "memv_p", "shift_s")}
    dh = d // N_XHEADS
    for l in range(depth):
        lpk = _layer_params(l, w)
        mk, mv = norm_mm(mem2, lpk["g_mem"], lpk["w_kv"], (d, d), PROMPT_TM)
        mk, mv = mk.reshape(bp, nm, d), mv.reshape(bp, nm, d)
        xp, s_p, sh_p, cb_p = _prompt_layer(xp, mk, mv, jnp.zeros((bp, wr), F32),
                                            jnp.zeros((bp, CONV_WIDTH - 1, c), F32), lpk)
        xs, s_all, sh_s, buf_all = _sample_layer(xs, cache_mem_k, cache_mem_v, s_all, state_shift[l], buf_all,
                                                 l, lpk)
        outs["rwkv_p"].append(s_p); outs["shift_p"].append(sh_p); outs["conv_p"].append(cb_p)
        outs["memk_p"].append(mk.reshape(bp, nm, N_XHEADS, dh)); outs["memv_p"].append(mv.reshape(bp, nm, N_XHEADS, dh))
        outs["shift_s"].append(sh_s)
    st = {n: jnp.stack(v) for n, v in outs.items()}
    return (xp, xs.reshape(bs, 1, d), st["rwkv_p"], st["shift_p"], st["conv_p"], st["memk_p"], st["memv_p"],
            s_all, st["shift_s"], buf_all)
```

```python
import functools
import math

import jax
import jax.numpy as jnp
from jax import lax
from jax.experimental import pallas as pl
from jax.experimental.pallas import tpu as pltpu

F32 = jnp.float32
BF16 = jnp.bfloat16

HEAD_SIZE = 64
D_DECAY_LORA = 64
D_ICLR_LORA = 64
D_GATE_LORA = 128
CONV_WIDTH = 31
N_XHEADS = 4
GN_EPS = 64e-5
NORM_EPS = 1e-6
LN_EPS = 1e-5

LANES = 128
SUBLANES = 8
VMEM_LIMIT_BYTES = 56 * 1024 * 1024

WKV_CHUNK = 64
WKV_SEQS = 4


def _cparams(*sem):
    return pltpu.CompilerParams(dimension_semantics=sem, vmem_limit_bytes=VMEM_LIMIT_BYTES)


def _dot(a, b):
    return jnp.dot(a.astype(BF16), b.astype(BF16), preferred_element_type=F32)


def _dot_nt(a, b):
    return lax.dot_general(a.astype(BF16), b.astype(BF16), (((1,), (1,)), ((), ())),
                           preferred_element_type=F32)


def _split3(x):
    hi = x.astype(BF16)
    r1 = x - hi.astype(F32)
    mid = r1.astype(BF16)
    lo = (r1 - mid.astype(F32)).astype(BF16)
    return hi, mid, lo


def _segsum(x, bd):
    return jnp.dot(x.astype(BF16), bd, preferred_element_type=F32)


def _rms(x, g):
    return x * lax.rsqrt(jnp.mean(x * x, axis=-1, keepdims=True) + NORM_EPS) * g


def _sigmoid(x):
    return 0.5 * jnp.tanh(0.5 * x) + 0.5


def _row_tile(m, target):
    t = min(m, target)
    assert m % t == 0, (m, t)
    return t


def _full(shape):
    nd = len(shape)
    return pl.BlockSpec(shape, lambda *_: (0,) * nd)


def _norm_mm_kernel(x_ref, g_ref, w_ref, *o_refs, splits):
    h = _rms(x_ref[...], g_ref[...]).astype(BF16)
    off = 0
    for o_ref, n in zip(o_refs, splits):
        o_ref[...] = jnp.dot(h, w_ref[:, off:off + n], preferred_element_type=F32)
        off += n


def norm_mm(x, g, w, splits, tm):
    m, k = x.shape
    tm = _row_tile(m, tm)
    n_all = w.shape[1]
    assert sum(splits) == n_all and all(s % LANES == 0 for s in splits)
    return pl.pallas_call(
        functools.partial(_norm_mm_kernel, splits=tuple(splits)),
        grid=(m // tm,),
        in_specs=[pl.BlockSpec((tm, k), lambda i: (i, 0)), _full((1, k)), _full((k, n_all))],
        out_specs=[pl.BlockSpec((tm, n), lambda i: (i, 0)) for n in splits],
        out_shape=[jax.ShapeDtypeStruct((m, n), F32) for n in splits],
        compiler_params=_cparams("parallel"),
        name="norm_mm",
    )(x, g.reshape(1, k), w)


def _proj_in_kernel(x_ref, sh_ref, gpre_ref, w_ref, mu_ref, w0_ref, w2_ref, a0_ref, a2_ref, g2_ref, kk_ref,
                    ka_ref, bd_ref, gb_ref, mb_ref, r_ref, lw_ref, k_ref, v_ref, a_ref, b_ref, g_ref, glu_ref,
                    gate_ref, last_ref, prev_scr, *, c, tiles_per_seq):
    if tiles_per_seq:
        @pl.when(pl.program_id(0) == 0)
        def _():
            prev_scr[...] = jnp.zeros_like(prev_scr)

    h = _rms(x_ref[...], gpre_ref[...]).astype(BF16)
    wr = 3 * c + D_DECAY_LORA + D_ICLR_LORA + D_GATE_LORA
    p = jnp.dot(h, w_ref[:, :wr], preferred_element_type=F32)
    u = jnp.dot(h, w_ref[:, wr:wr + 2 * c], preferred_element_type=F32) + gb_ref[...]
    glu_ref[...] = u[:, :c] * _sigmoid(u[:, c:])
    gate_ref[...] = _sigmoid(jnp.dot(h, w_ref[:, wr + 2 * c:], preferred_element_type=F32) + mb_ref[...])
    if tiles_per_seq == 0:
        pp = sh_ref[...]
        last_ref[...] = p
    else:
        tm = p.shape[0]
        first = (pl.program_id(0) % tiles_per_seq) == 0
        prev_row = jnp.where(first, sh_ref[0], prev_scr[...])
        rows = lax.broadcasted_iota(jnp.int32, p.shape, 0)
        pp = jnp.where(rows == 0, prev_row, pltpu.roll(p, 1, axis=0))
        prev_scr[...] = p[tm - 1:tm, :]
        last_ref[0] = p[tm - 1:tm, :]
    xs = p + (pp - p) * mu_ref[...]
    r = xs[:, 0:c]
    k = xs[:, c:2 * c]
    v = xs[:, 2 * c:3 * c]
    wa = xs[:, 3 * c:3 * c + D_DECAY_LORA + D_ICLR_LORA]
    gd = xs[:, 3 * c + D_DECAY_LORA + D_ICLR_LORA:]
    z = w0_ref[...] + _dot(jnp.tanh(wa), w2_ref[...])
    lw_ref[...] = -math.exp(-0.5) * _sigmoid(z)
    a = _sigmoid(a0_ref[...] + _dot(wa, a2_ref[...]))
    g_ref[...] = _dot(_sigmoid(gd), g2_ref[...])
    kk = k * kk_ref[...]
    ss = _segsum(kk * kk, bd_ref[...])
    kk = kk * lax.rsqrt(jnp.maximum(ss, 1e-24))
    r_ref[...] = r
    k_ref[...] = k * (1.0 + (a - 1.0) * ka_ref[...])
    v_ref[...] = v
    a_ref[...] = -kk
    b_ref[...] = kk * a


def proj_in(x, shift0, seq_len, lpk, tm):
    m, d = x.shape
    c = lpk["c_rwkv"]
    wr = shift0.shape[-1]
    nseq = m // seq_len
    tm = _row_tile(m, tm)
    row = lambda n: pl.BlockSpec((tm, n), lambda i: (i, 0))
    if seq_len == 1:
        tiles_per_seq, sh, sh_spec = 0, shift0, row(wr)
        last_spec, last_shape = row(wr), (m, wr)
    else:
        assert seq_len % tm == 0
        tiles_per_seq = seq_len // tm
        sh = shift0.reshape(nseq, 1, wr)
        sh_spec = pl.BlockSpec((1, 1, wr), lambda i: (i // tiles_per_seq, 0, 0))
        last_spec, last_shape = sh_spec, (nseq, 1, wr)
    names = ("g_mix_pre", "w_in", "tshift_mu", "decay_w0", "decay_w2p", "iclr_a0", "iclr_a2p", "gate_g2", "k_k",
             "k_a", "bd", "glu_b", "merge_gate_b")
    params = [lpk[n] for n in names]
    outs = pl.pallas_call(
        functools.partial(_proj_in_kernel, c=c, tiles_per_seq=tiles_per_seq),
        grid=(m // tm,),
        in_specs=[row(d), sh_spec] + [_full(a.shape) for a in params],
        out_specs=[row(c)] * 8 + [row(2 * d), last_spec],
        out_shape=[jax.ShapeDtypeStruct((m, c), F32)] * 8 + [jax.ShapeDtypeStruct((m, 2 * d), F32),
                                                             jax.ShapeDtypeStruct(last_shape, F32)],
        scratch_shapes=[pltpu.VMEM((1, wr), F32)],
        compiler_params=_cparams("arbitrary"),
        name="proj_in",
    )(x, sh, *params)
    return outs[:9], outs[9].reshape(nseq, wr)


def _wkv_chunk_kernel(r_ref, lw_ref, k_ref, v_ref, a_ref, b_ref, y_ref, s_ref, g_scr, *, L, npairs, nb):
    ci = pl.program_id(1)

    @pl.when(ci == 0)
    def _():
        g_scr[...] = jnp.zeros_like(g_scr)

    row = lax.broadcasted_iota(jnp.int32, (L, L), 0)
    col = lax.broadcasted_iota(jnp.int32, (L, L), 1)
    ltri = jnp.where(col <= row, 1.0, 0.0).astype(BF16)
    d = functools.partial(jnp.dot, preferred_element_type=F32)
    m0 = lax.broadcasted_iota(jnp.int32, (L, LANES), 1) < HEAD_SIZE
    i2 = lax.broadcasted_iota(jnp.int32, (2 * L, 2 * L), 0)
    j2 = lax.broadcasted_iota(jnp.int32, (2 * L, 2 * L), 1)
    same = (i2 < L) == (j2 < L)
    tril_s = same & (j2 < i2)
    tril_i = same & (j2 <= i2)
    eye = jnp.where(i2 == j2, 1.0, 0.0)

    def stack(x, p):
        xp = x[:, p * LANES:(p + 1) * LANES]
        return jnp.concatenate([jnp.where(m0, xp, 0.0), jnp.where(m0, 0.0, xp)], axis=0)

    inst = [(j, p) for j in range(nb) for p in range(npairs)]
    lhs, rhs, vm, p_last = [], [], [], []
    for j in range(nb):
        lw = lw_ref[j]
        hi, mid, lo = _split3(lw)
        cs = d(ltri, hi) + (d(ltri, mid) + d(ltri, lo))
        pin = jnp.exp(cs)
        pinv = jnp.exp(-cs)
        at = a_ref[j] * jnp.exp(cs - lw)
        rt = r_ref[j] * pin
        bt = b_ref[j] * pinv
        kt = k_ref[j] * pinv
        v = v_ref[j]
        for p in range(npairs):
            lhs.append(jnp.concatenate([stack(at, p), stack(rt, p)], axis=0).astype(BF16))
            rhs.append(jnp.concatenate([stack(bt, p), stack(kt, p)], axis=0))
            vm.append(stack(v, p).astype(BF16))
            p_last.append(pin[L - 1:L, p * LANES:(p + 1) * LANES])
    n = len(inst)
    gm = [_dot_nt(lhs[i], rhs[i]) for i in range(n)]
    a_ab = [jnp.where(tril_s, gm[i][:2 * L, :2 * L], 0.0) for i in range(n)]
    a_kr = [jnp.concatenate([jnp.where(tril_s, gm[i][:2 * L, 2 * L:], 0.0),
                             jnp.where(tril_i, gm[i][2 * L:, 2 * L:], 0.0)], axis=0) for i in range(n)]
    a_rb = [jnp.where(tril_i, gm[i][2 * L:, :2 * L], 0.0) for i in range(n)]
    rhs_t = [rhs[i].T.astype(BF16) for i in range(n)]
    decay = [jnp.broadcast_to(p_last[i], (LANES, LANES)).T for i in range(n)]
    lhs2 = [jnp.concatenate([lhs[i], a_kr[i].astype(BF16)], axis=1) for i in range(n)]
    t = [eye + a_ab[i] for i in range(n)]
    ak = [_dot(a_ab[i], a_ab[i]) for i in range(n)]
    for _ in range(int(math.log2(L)) - 2):
        pr = [_dot(ak[i], jnp.concatenate([ak[i], t[i]], axis=1)) for i in range(n)]
        ak = [pr[i][:, :2 * L] for i in range(n)]
        t = [t[i] + pr[i][:, 2 * L:] for i in range(n)]
    t = [t[i] + _dot(ak[i], t[i]) for i in range(n)]
    tc = [jnp.concatenate([t[i], _dot(a_rb[i], t[i])], axis=0) for i in range(n)]
    h = [g_scr[j, p] for (j, p) in inst]
    xy = [_dot(lhs2[i], jnp.concatenate([h[i].astype(BF16), vm[i]], axis=0)) for i in range(n)]
    uy = [_dot(tc[i], xy[i][:2 * L]) for i in range(n)]
    for i, (j, p) in enumerate(inst):
        yy = xy[i][2 * L:] + uy[i][2 * L:]
        y_ref[j, :, p * LANES:(p + 1) * LANES] = yy[:L] + yy[L:]
        uv = jnp.concatenate([uy[i][:2 * L].astype(BF16), vm[i]], axis=0)
        g_scr[j, p] = (h[i] + _dot(rhs_t[i], uv)) * decay[i]

    @pl.when(ci == pl.num_programs(1) - 1)
    def _():
        for j, p in inst:
            s_ref[j, p] = g_scr[j, p].T


def wkv_chunked(r, lw, k, v, a, b, nb=1):
    bsz, t, c = r.shape
    L = WKV_CHUNK
    assert t % L == 0 and c % LANES == 0 and bsz % nb == 0
    npairs = c // LANES
    seq = pl.BlockSpec((nb, L, c), lambda bi, ci: (bi, ci, 0))
    y, s = pl.pallas_call(
        functools.partial(_wkv_chunk_kernel, L=L, npairs=npairs, nb=nb),
        grid=(bsz // nb, t // L),
        in_specs=[seq] * 6,
        out_specs=[seq, pl.BlockSpec((nb, npairs, LANES, LANES), lambda bi, ci: (bi, 0, 0, 0))],
        out_shape=[jax.ShapeDtypeStruct((bsz, t, c), F32),
                   jax.ShapeDtypeStruct((bsz, npairs, LANES, LANES), F32)],
        scratch_shapes=[pltpu.VMEM((nb, npairs, LANES, LANES), F32)],
        compiler_params=_cparams("parallel", "arbitrary"),
        name="wkv_chunked",
    )(r, lw, k, v, a, b)
    n = HEAD_SIZE
    s = jnp.stack([s[:, :, :n, :n], s[:, :, n:, n:]], axis=2)
    return y, s.reshape(bsz, 2 * npairs, n, n)


def _wkv_step_kernel(*refs, bb, nh, chained):
    s_ref, r_ref, lw_ref, k_ref, v_ref, a_ref, b_ref, so_ref, y_ref = refs[1:] if chained else refs
    n = HEAD_SIZE
    eye = jnp.where(lax.broadcasted_iota(jnp.int32, (n, n), 0) == lax.broadcasted_iota(jnp.int32, (n, n), 1),
                    1.0, 0.0)
    pad = jnp.zeros((SUBLANES - 3, n), F32)
    inst = [(i, h) for i in range(bb) for h in range(nh)]
    hs = [slice(h, h + 1) for h in range(nh)]
    r, k, v, a, b = ([ref[i] for i in range(bb)] for ref in (r_ref, k_ref, v_ref, a_ref, b_ref))
    w = [jnp.exp(lw_ref[i]) for i in range(bb)]
    wr = [w[i] * r[i] for i in range(bb)]
    br = [jnp.sum(b[i] * r[i], axis=1, keepdims=True) for i in range(bb)]
    kr = [jnp.sum(k[i] * r[i], axis=1, keepdims=True) for i in range(bb)]
    s = [s_ref[i, h] for i, h in inst]
    x8 = [jnp.concatenate([a[i][hs[h]], wr[i][hs[h]], v[i][hs[h]], pad], axis=0) for i, h in inst]
    rows = [_dot_nt(x8[j], s[j]) for j in range(len(inst))]
    se = [jnp.concatenate([s[j], eye], axis=0).astype(BF16) for j in range(len(inst))]
    x8s = [_split3(x8[j]) for j in range(len(inst))]
    cols = [_dot_nt(se[j], x8s[j][0]) + (_dot_nt(se[j], x8s[j][1]) + _dot_nt(se[j], x8s[j][2]))
            for j in range(len(inst))]
    ys = []
    for j, (i, h) in enumerate(inst):
        sa_col = cols[j][:n, 0:1]
        v_col = cols[j][n:, 2:3]
        so_ref[i, h] = s[j] * w[i][hs[h]] + sa_col * b[i][hs[h]] + v_col * k[i][hs[h]]
        ys.append(rows[j][1:2] + rows[j][0:1] * br[i][hs[h]] + v[i][hs[h]] * kr[i][hs[h]])
    for i in range(bb):
        y_ref[i] = jnp.concatenate(ys[i * nh:(i + 1) * nh], axis=0)


def _chain_out(prev, shape):
    if prev is None:
        return [], [], {}
    assert prev.shape == tuple(shape)
    return [pl.BlockSpec(memory_space=pl.ANY)], [prev], {0: 0}


def wkv_step(s_in, s_out, layer, r, lw, k, v, a, b):
    _, bsz, nh, n, _ = s_in.shape
    bb = _row_tile(bsz, 4)
    vec = pl.BlockSpec((bb, nh, n), lambda i: (i, 0, 0))
    st = pl.BlockSpec((None, bb, nh, n, n), lambda i: (layer, i, 0, 0, 0))
    hv = lambda x: x.reshape(bsz, nh, n)
    c_specs, c_ops, c_alias = _chain_out(s_out, s_in.shape)
    s_out, y = pl.pallas_call(
        functools.partial(_wkv_step_kernel, bb=bb, nh=nh, chained=bool(c_ops)),
        grid=(bsz // bb,),
        in_specs=c_specs + [st] + [vec] * 6,
        out_specs=[st, vec],
        out_shape=[jax.ShapeDtypeStruct(s_in.shape, F32), jax.ShapeDtypeStruct((bsz, nh, n), F32)],
        input_output_aliases=c_alias,
        compiler_params=_cparams("parallel"),
        name="wkv_step",
    )(*c_ops, s_in, hv(r), hv(lw), hv(k), hv(v), hv(a), hv(b))
    return y.reshape(bsz, nh * n), s_out


CONV_PAD = 32


CONV_ROWS = 32


def _ln_swish(y, g_ref, b_ref):
    mu = jnp.mean(y, axis=-1, keepdims=True)
    dlt = y - mu
    var = jnp.mean(dlt * dlt, axis=-1, keepdims=True)
    y = dlt * lax.rsqrt(var + LN_EPS) * g_ref[...] + b_ref[...]
    return y * _sigmoid(y)


def _conv_seq_kernel(u_ref, buf_ref, cw_ref, cb_ref, lg_ref, lb_ref, o_ref, nb_ref, xp_scr, *, tt):
    ti = pl.program_id(1)

    @pl.when(ti == 0)
    def _():
        xp_scr[0:CONV_PAD, :] = buf_ref[0]

    xp_scr[CONV_PAD:CONV_PAD + tt, :] = u_ref[0]
    base = CONV_PAD - (CONV_WIDTH - 1)
    rb = CONV_ROWS

    for i in range(tt // rb):
        s0 = i * rb
        acc = None
        for r in range(SUBLANES):
            rows = rb if r == 0 else rb + SUBLANES
            part = None
            for j in range((CONV_PAD + SUBLANES) // SUBLANES):
                w = SUBLANES * j + r - base
                if 0 <= w < CONV_WIDTH:
                    term = xp_scr[pl.ds(s0 + SUBLANES * j, rows), :] * cw_ref[w:w + 1, :]
                    part = term if part is None else part + term
            part = part if r == 0 else part[r:r + rb]
            acc = part if acc is None else acc + part
        o_ref[0, pl.ds(s0, rb), :] = _ln_swish(acc + cb_ref[...], lg_ref, lb_ref)
    tail = xp_scr[tt:tt + CONV_PAD, :]
    xp_scr[0:CONV_PAD, :] = tail

    @pl.when(ti == pl.num_programs(1) - 1)
    def _():
        nb_ref[0] = tail


def conv_seq(u, buf0, lpk, tt):
    bsz, t, c = u.shape
    tt = _row_tile(t, tt)
    assert tt % CONV_ROWS == 0
    pad = CONV_PAD - (CONV_WIDTH - 1)
    bufp = jnp.pad(buf0, ((0, 0), (pad, 0), (0, 0)))
    params = [lpk[n] for n in ("conv_w", "conv_b", "conv_ln_g", "conv_ln_b")]
    hist = pl.BlockSpec((1, CONV_PAD, c), lambda bi, ti: (bi, 0, 0))
    out, nb = pl.pallas_call(
        functools.partial(_conv_seq_kernel, tt=tt),
        grid=(bsz, t // tt),
        in_specs=[pl.BlockSpec((1, tt, c), lambda bi, ti: (bi, ti, 0)), hist] + [_full(a.shape) for a in params],
        out_specs=[pl.BlockSpec((1, tt, c), lambda bi, ti: (bi, ti, 0)), hist],
        out_shape=[jax.ShapeDtypeStruct((bsz, t, c), F32), jax.ShapeDtypeStruct((bsz, CONV_PAD, c), F32)],
        scratch_shapes=[pltpu.VMEM((CONV_PAD + tt, c), F32)],
        compiler_params=_cparams("parallel", "arbitrary"),
        name="conv_seq",
    )(u, bufp, *params)
    return out, nb[:, pad:]


def _conv_step_kernel(*refs, bb, chained):
    u_ref, buf_ref, cw_ref, cb_ref, lg_ref, lb_ref, nb_ref, o_ref = refs[1:] if chained else refs
    hist = CONV_WIDTH - 1
    glu = u_ref[...]
    w_hist = cw_ref[0:hist, :]
    w_last = cw_ref[hist:hist + 1, :]
    rows = []
    for i in range(bb):
        rows.append(jnp.sum(buf_ref[i] * w_hist, axis=0, keepdims=True))
    acc = jnp.concatenate(rows, axis=0) + glu * w_last + cb_ref[...]
    o_ref[...] = _ln_swish(acc, lg_ref, lb_ref)
    for i in range(bb):
        nb_ref[i, 0:hist - 1, :] = buf_ref[i, 1:hist, :]
        nb_ref[i, hist - 1:hist, :] = glu[i:i + 1, :]


def conv_step(u, buf_in, buf_out, layer, lpk):
    bsz, c = u.shape
    bb = _row_tile(bsz, 8)
    params = [lpk[n] for n in ("conv_w", "conv_b", "conv_ln_g", "conv_ln_b")]
    hist = pl.BlockSpec((None, bb, CONV_WIDTH - 1, c), lambda i: (layer, i, 0, 0))
    c_specs, c_ops, c_alias = _chain_out(buf_out, buf_in.shape)
    buf_out, out = pl.pallas_call(
        functools.partial(_conv_step_kernel, bb=bb, chained=bool(c_ops)),
        grid=(bsz // bb,),
        in_specs=c_specs + [pl.BlockSpec((bb, c), lambda i: (i, 0)), hist] + [_full(a.shape) for a in params],
        out_specs=[hist, pl.BlockSpec((bb, c), lambda i: (i, 0))],
        out_shape=[jax.ShapeDtypeStruct(buf_in.shape, F32), jax.ShapeDtypeStruct((bsz, c), F32)],
        input_output_aliases=c_alias,
        compiler_params=_cparams("parallel"),
        name="conv_step",
    )(*c_ops, u, buf_in, *params)
    return out, buf_out


def _merge_kernel(x_ref, y_ref, r_ref, k_ref, v_ref, g_ref, cv_ref, gate_ref, bd_ref, lg_ref, lb_ref, rk_ref,
                  wa_ref, wp_ref, bp_ref, wo_ref, gp_ref, o_ref, *, d):
    bd = bd_ref[...]
    inv_n = 1.0 / HEAD_SIZE
    y = y_ref[...]
    mu = _segsum(y, bd) * inv_n
    dlt = y - mu
    var = _segsum(dlt * dlt, bd) * inv_n
    yn = dlt * lax.rsqrt(var + GN_EPS) * lg_ref[...] + lb_ref[...]
    v = v_ref[...]
    bonus = _segsum(r_ref[...] * k_ref[...] * rk_ref[...], bd) * v
    za = (yn + bonus) * g_ref[...]
    ya = _dot(za, wa_ref[...])
    yb = _dot(cv_ref[...], wp_ref[...]) + bp_ref[...]
    mixed = _dot(gate_ref[:, :d] * ya + gate_ref[:, d:] * yb, wo_ref[...])
    o_ref[...] = x_ref[...] + _rms(mixed, gp_ref[...])


def merge(x, y, r, k, v, g, cv, gate, lpk, tm):
    m, d = x.shape
    c = y.shape[1]
    tm = _row_tile(m, tm)
    row = lambda n: pl.BlockSpec((tm, n), lambda i: (i, 0))
    names = ("bd", "lnx_g", "lnx_b", "r_k", "w_out_rwkv", "w_pw", "b_pw", "w_out", "g_mix_post")
    params = [lpk[n] for n in names]
    return pl.pallas_call(
        functools.partial(_merge_kernel, d=d),
        grid=(m // tm,),
        in_specs=[row(d)] + [row(c)] * 6 + [row(2 * d)] + [_full(a.shape) for a in params],
        out_specs=row(d),
        out_shape=jax.ShapeDtypeStruct((m, d), F32),
        compiler_params=_cparams("parallel"),
        name="merge",
    )(x, y, r, k, v, g, cv, gate, *params)


def _xattn_kernel(x_ref, mk_ref, mv_ref, gq_ref, wq_ref, wo_ref, gp_ref, o_ref, mk_scr, mv_scr, *, dh):
    @pl.when(pl.program_id(1) == 0)
    def _():
        mk_scr[...] = mk_ref[0].astype(BF16)
        mv_scr[...] = mv_ref[0].astype(BF16)

    x = x_ref[0]
    q = _dot(_rms(x, gq_ref[...]), wq_ref[...])
    scale = dh ** -0.5
    outs = []
    for h in range(N_XHEADS):
        sl = slice(h * dh, (h + 1) * dh)
        s = _dot_nt(q[:, sl], mk_scr[:, sl]) * scale
        s = s - jnp.max(s, axis=-1, keepdims=True)
        e = jnp.exp(s)
        pr = e / jnp.sum(e, axis=-1, keepdims=True)
        outs.append(_dot(pr, mv_scr[:, sl]))
    o = jnp.concatenate(outs, axis=1)
    o_ref[0] = x + _rms(_dot(o, wo_ref[...]), gp_ref[...])


def xattn(x, mk, mv, lpk, tq):
    bsz, t, d = x.shape
    nm = mk.shape[1]
    tq = _row_tile(t, tq)
    params = [lpk[n] for n in ("g_x_pre", "w_q", "w_o_x", "g_x_post")]
    xs = pl.BlockSpec((1, tq, d), lambda bi, ti: (bi, ti, 0))
    ms = pl.BlockSpec((1, nm, d), lambda bi, ti: (bi, 0, 0))
    return pl.pallas_call(
        functools.partial(_xattn_kernel, dh=d // N_XHEADS),
        grid=(bsz, t // tq),
        in_specs=[xs, ms, ms] + [_full(a.shape) for a in params],
        out_specs=xs,
        out_shape=jax.ShapeDtypeStruct(x.shape, F32),
        scratch_shapes=[pltpu.VMEM((nm, d), BF16), pltpu.VMEM((nm, d), BF16)],
        compiler_params=_cparams("parallel", "arbitrary"),
        name="xattn",
    )(x, mk, mv, *params)


def _xattn_step_kernel(q_ref, k_ref, v_ref, o_ref, *, bb, nm, nh, dh):
    assert 2 * nh == SUBLANES
    g = nm * nh // SUBLANES
    for j in range(bb):
        q = q_ref[j]
        q8 = jnp.concatenate([q, q], axis=0)
        kf = k_ref[j].reshape(nm * nh, dh).reshape(g, SUBLANES, dh)
        vf = v_ref[j].reshape(nm * nh, dh).reshape(g, SUBLANES, dh)
        s = jnp.sum(kf * q8[None], axis=-1, keepdims=True) * dh ** -0.5
        mx = jnp.max(s, axis=0, keepdims=True)
        mx = jnp.maximum(mx, pltpu.roll(mx[0], nh, axis=0)[None])
        e = jnp.exp(s - mx)
        den = jnp.sum(e, axis=0, keepdims=True)
        den = den + pltpu.roll(den[0], nh, axis=0)[None]
        o8 = jnp.sum((e / den) * vf, axis=0)
        o_ref[j] = o8[:nh] + o8[nh:]


def xattn_step(q, cache_k, cache_v, layer):
    bsz, d = q.shape
    _, _, nm, nh, dh = cache_k.shape
    bb = _row_tile(bsz, 2)
    qs = pl.BlockSpec((bb, nh, dh), lambda i: (i, 0, 0))
    cs = pl.BlockSpec((None, bb, nm, nh, dh), lambda i: (layer, i, 0, 0, 0))
    o = pl.pallas_call(
        functools.partial(_xattn_step_kernel, bb=bb, nm=nm, nh=nh, dh=dh),
        grid=(bsz // bb,),
        in_specs=[qs, cs, cs],
        out_specs=qs,
        out_shape=jax.ShapeDtypeStruct((bsz, nh, dh), F32),
        compiler_params=_cparams("parallel"),
        name="xattn_step",
    )(q.reshape(bsz, nh, dh), cache_k, cache_v)
    return o.reshape(bsz, d)


def _mm_norm_res_kernel(a_ref, x_ref, w_ref, g_ref, o_ref):
    o_ref[...] = x_ref[...] + _rms(_dot(a_ref[...], w_ref[...]), g_ref[...])


def mm_norm_res(a, x, w, g, tm):
    m, k = a.shape
    n = w.shape[1]
    tm = _row_tile(m, tm)
    return pl.pallas_call(
        _mm_norm_res_kernel,
        grid=(m // tm,),
        in_specs=[pl.BlockSpec((tm, k), lambda i: (i, 0)), pl.BlockSpec((tm, n), lambda i: (i, 0)),
                  _full((k, n)), _full((1, n))],
        out_specs=pl.BlockSpec((tm, n), lambda i: (i, 0)),
        out_shape=jax.ShapeDtypeStruct((m, n), F32),
        compiler_params=_cparams("parallel"),
        name="mm_norm_res",
    )(a, x, w, g)


MXU_K = 256


def _ffn_chunks(dff):
    half = -(-(dff // 2) // MXU_K) * MXU_K
    return ((0, half), (half, dff)) if half < dff else ((0, dff),)


def _ffn_kernel(x_ref, gpre_ref, wgu_ref, wd_ref, gpost_ref, o_ref, *, dff):
    x = x_ref[...]
    h = _rms(x, gpre_ref[...]).astype(BF16)
    acc = None
    for lo, hi in _ffn_chunks(dff):
        gt = jnp.dot(h, wgu_ref[:, lo:hi], preferred_element_type=F32)
        up = jnp.dot(h, wgu_ref[:, dff + lo:dff + hi], preferred_element_type=F32)
        part = _dot(gt * _sigmoid(gt) * up, wd_ref[lo:hi, :])
        acc = part if acc is None else acc + part
    o_ref[...] = x + _rms(acc, gpost_ref[...])


def _resident(shape):
    nd = len(shape)
    return pl.BlockSpec(shape, lambda *_: (0,) * nd, pipeline_mode=pl.Buffered(1))


def ffn(x, lpk, tm):
    m, d = x.shape
    w_gu, w_dn = lpk["w_gate_up"], lpk["w_down"]
    dff = w_dn.shape[0]
    assert dff % LANES == 0
    tm = _row_tile(m, tm)
    return pl.pallas_call(
        functools.partial(_ffn_kernel, dff=dff),
        grid=(m // tm,),
        in_specs=[pl.BlockSpec((tm, d), lambda i: (i, 0)), _full((1, d)), _resident(w_gu.shape),
                  _resident(w_dn.shape), _full((1, d))],
        out_specs=pl.BlockSpec((tm, d), lambda i: (i, 0)),
        out_shape=jax.ShapeDtypeStruct((m, d), F32),
        compiler_params=_cparams("parallel"),
        name="ffn",
    )(x, lpk["g_ffn_pre"], w_gu, w_dn, lpk["g_ffn_post"])


def _layer_params(l, w):
    row = lambda a: a[l].reshape(1, -1)
    c = w["w_out_rwkv"].shape[1]
    lora_rows = D_DECAY_LORA + D_ICLR_LORA
    w2p = jnp.zeros((lora_rows, c), F32).at[:D_DECAY_LORA].set(w["decay_w2"][l])
    a2p = jnp.zeros((lora_rows, c), F32).at[D_DECAY_LORA:].set(w["iclr_a2"][l])
    head = jnp.arange(c) // HEAD_SIZE
    lpk = {
        "c_rwkv": c,
        "bd": (head[:, None] == head[None, :]).astype(BF16),
        "decay_w2p": w2p.astype(BF16), "iclr_a2p": a2p.astype(BF16),
        "conv_w": jnp.pad(w["conv_w"][l], ((0, CONV_PAD - CONV_WIDTH), (0, 0))),
    }
    for n in ("w_in", "gate_g2", "w_out_rwkv", "w_pw", "w_out", "w_q", "w_o_x", "w_gate_up", "w_down"):
        lpk[n] = w[n][l].astype(BF16)
    lpk["w_kv"] = jnp.concatenate([w["w_k"][l], w["w_v"][l]], axis=1).astype(BF16)
    for n in ("tshift_mu", "decay_w0", "iclr_a0", "k_k", "k_a", "r_k", "lnx_g", "lnx_b", "glu_b", "conv_b",
              "conv_ln_g", "conv_ln_b", "b_pw", "merge_gate_b", "g_mix_pre", "g_mix_post", "g_mem", "g_x_pre",
              "g_x_post", "g_ffn_pre", "g_ffn_post"):
        lpk[n] = row(w[n])
    return lpk


PROMPT_TM = 256
SAMPLE_TM = 128


def _prompt_layer(x, mem_k, mem_v, shift0, buf0, lpk):
    bsz, t, d = x.shape
    m = bsz * t
    c = lpk["c_rwkv"]
    x2 = x.reshape(m, d)
    (r, lw, k, v, a, b, g, glu, gate), shift1 = proj_in(x2, shift0, t, lpk, PROMPT_TM)
    seq = lambda z: z.reshape(bsz, t, c)
    y, s1 = wkv_chunked(seq(r), seq(lw), seq(k), seq(v), seq(a), seq(b), nb=math.gcd(bsz, WKV_SEQS))
    cv, buf1 = conv_seq(seq(glu), buf0, lpk, 256)
    x2 = merge(x2, y.reshape(m, c), r, k, v, g, cv.reshape(m, c), gate, lpk, PROMPT_TM)
    x2 = xattn(x2.reshape(bsz, t, d), mem_k, mem_v, lpk, 512 if t % 512 == 0 else 256).reshape(m, d)
    x2 = ffn(x2, lpk, 2 * PROMPT_TM)
    return x2.reshape(bsz, t, d), s1, shift1, buf1


def _sample_layer(x2, cache_k, cache_v, s_in, s_out, shift0, buf_in, buf_out, layer, lpk):
    d = x2.shape[1]
    (r, lw, k, v, a, b, g, glu, gate), shift1 = proj_in(x2, shift0, 1, lpk, SAMPLE_TM)
    y, s_out = wkv_step(s_in, s_out, layer, r, lw, k, v, a, b)
    cv, buf_out = conv_step(glu, buf_in, buf_out, layer, lpk)
    x2 = merge(x2, y, r, k, v, g, cv, gate, lpk, SAMPLE_TM)
    (q,) = norm_mm(x2, lpk["g_x_pre"], lpk["w_q"], (d,), SAMPLE_TM)
    o = xattn_step(q, cache_k, cache_v, layer)
    x2 = mm_norm_res(o, x2, lpk["w_o_x"], lpk["g_x_post"], SAMPLE_TM)
    x2 = ffn(x2, lpk, SAMPLE_TM)
    return x2, s_out, shift1, buf_out


def kernel(x_prompt, x_sample, mem_prompt, state_rwkv, state_shift, state_conv, cache_mem_k, cache_mem_v, w_in, tshift_mu, decay_w0, decay_w2, iclr_a0, iclr_a2, gate_g2, k_k, k_a, r_k, lnx_g, lnx_b, w_out_rwkv, glu_b, conv_w, conv_b, conv_ln_g, conv_ln_b, w_pw, b_pw, merge_gate_b, w_out, g_mix_pre, g_mix_post, g_mem, w_q, w_k, w_v, w_o_x, g_x_pre, g_x_post, g_ffn_pre, g_ffn_post, w_gate_up, w_down):
    w = dict(w_in=w_in, tshift_mu=tshift_mu, decay_w0=decay_w0, decay_w2=decay_w2, iclr_a0=iclr_a0,
             iclr_a2=iclr_a2, gate_g2=gate_g2, k_k=k_k, k_a=k_a, r_k=r_k, lnx_g=lnx_g, lnx_b=lnx_b,
             w_out_rwkv=w_out_rwkv, glu_b=glu_b, conv_w=conv_w, conv_b=conv_b, conv_ln_g=conv_ln_g,
             conv_ln_b=conv_ln_b, w_pw=w_pw, b_pw=b_pw, merge_gate_b=merge_gate_b, w_out=w_out,
             g_mix_pre=g_mix_pre, g_mix_post=g_mix_post, g_mem=g_mem, w_q=w_q, w_k=w_k, w_v=w_v, w_o_x=w_o_x,
             g_x_pre=g_x_pre, g_x_post=g_x_post, g_ffn_pre=g_ffn_pre, g_ffn_post=g_ffn_post,
             w_gate_up=w_gate_up, w_down=w_down)
    depth = w_in.shape[0]
    bp, _, d = x_prompt.shape
    bs = x_sample.shape[0]
    nm = mem_prompt.shape[1]
    c = w_out_rwkv.shape[1]
    wr = state_shift.shape[-1]
    xp, xs = x_prompt, x_sample.reshape(bs, d)
    mem2 = mem_prompt.reshape(bp * nm, d)
    s_all, buf_all = None, None
    outs = {n: [] for n in ("rwkv_p", "shift_p", "conv_p", "memk_p", "memv_p", "shift_s")}
    dh = d // N_XHEADS
    for l in range(depth):
        lpk = _layer_params(l, w)
        mk, mv = norm_mm(mem2, lpk["g_mem"], lpk["w_kv"], (d, d), PROMPT_TM)
        mk, mv = mk.reshape(bp, nm, d), mv.reshape(bp, nm, d)
        xp, s_p, sh_p, cb_p = _prompt_layer(xp, mk, mv, jnp.zeros((bp, wr), F32),
                                            jnp.zeros((bp, CONV_WIDTH - 1, c), F32), lpk)
        xs, s_all, sh_s, buf_all = _sample_layer(xs, cache_mem_k, cache_mem_v, state_rwkv, s_all, state_shift[l],
                                                 state_conv, buf_all, l, lpk)
        outs["rwkv_p"].append(s_p); outs["shift_p"].append(sh_p); outs["conv_p"].append(cb_p)
        outs["memk_p"].append(mk.reshape(bp, nm, N_XHEADS, dh)); outs["memv_p"].append(mv.reshape(bp, nm, N_XHEADS, dh))
        outs["shift_s"].append(sh_s)
    st = {n: jnp.stack(v) for n, v in outs.items()}
    return (xp, xs.reshape(bs, 1, d), st["rwkv_p"], st["shift_p"], st["conv_p"], st["memk_p"], st["memv_p"],
            s_all, st["shift_s"], buf_all)
```

```python
import functools
import math

import jax
import jax.numpy as jnp
from jax import lax
from jax.experimental import pallas as pl
from jax.experimental.pallas import tpu as pltpu

F32 = jnp.float32
BF16 = jnp.bfloat16

HEAD_SIZE = 64
D_DECAY_LORA = 64
D_ICLR_LORA = 64
D_GATE_LORA = 128
CONV_WIDTH = 31
N_XHEADS = 4
GN_EPS = 64e-5
NORM_EPS = 1e-6
LN_EPS = 1e-5

LANES = 128
SUBLANES = 8
VMEM_LIMIT_BYTES = 56 * 1024 * 1024

WKV_CHUNK = 64
WKV_SEQS = 4


def _cparams(*sem):
    return pltpu.CompilerParams(dimension_semantics=sem, vmem_limit_bytes=VMEM_LIMIT_BYTES)


def _dot(a, b):
    return jnp.dot(a.astype(BF16), b.astype(BF16), preferred_element_type=F32)


def _dot_nt(a, b):
    return lax.dot_general(a.astype(BF16), b.astype(BF16), (((1,), (1,)), ((), ())),
                           preferred_element_type=F32)


def _split3(x):
    hi = x.astype(BF16)
    r1 = x - hi.astype(F32)
    mid = r1.astype(BF16)
    lo = (r1 - mid.astype(F32)).astype(BF16)
    return hi, mid, lo


def _segsum(x, bd):
    return jnp.dot(x.astype(BF16), bd, preferred_element_type=F32)


def _rms(x, g):
    return x * lax.rsqrt(jnp.mean(x * x, axis=-1, keepdims=True) + NORM_EPS) * g


def _sigmoid(x):
    return 0.5 * jnp.tanh(0.5 * x) + 0.5


def _row_tile(m, target):
    t = min(m, target)
    assert m % t == 0, (m, t)
    return t


def _full(shape):
    nd = len(shape)
    return pl.BlockSpec(shape, lambda *_: (0,) * nd)


def _norm_mm_kernel(x_ref, g_ref, w_ref, *o_refs, splits):
    h = _rms(x_ref[...], g_ref[...]).astype(BF16)
    off = 0
    for o_ref, n in zip(o_refs, splits):
        o_ref[...] = jnp.dot(h, w_ref[:, off:off + n], preferred_element_type=F32)
        off += n


def norm_mm(x, g, w, splits, tm):
    m, k = x.shape
    tm = _row_tile(m, tm)
    n_all = w.shape[1]
    assert sum(splits) == n_all and all(s % LANES == 0 for s in splits)
    return pl.pallas_call(
        functools.partial(_norm_mm_kernel, splits=tuple(splits)),
        grid=(m // tm,),
        in_specs=[pl.BlockSpec((tm, k), lambda i: (i, 0)), _full((1, k)), _full((k, n_all))],
        out_specs=[pl.BlockSpec((tm, n), lambda i: (i, 0)) for n in splits],
        out_shape=[jax.ShapeDtypeStruct((m, n), F32) for n in splits],
        compiler_params=_cparams("parallel"),
        name="norm_mm",
    )(x, g.reshape(1, k), w)


def _proj_in_kernel(x_ref, sh_ref, gpre_ref, w_ref, mu_ref, w0_ref, w2_ref, a0_ref, a2_ref, g2_ref, kk_ref,
                    ka_ref, bd_ref, gb_ref, mb_ref, r_ref, lw_ref, k_ref, v_ref, a_ref, b_ref, g_ref, glu_ref,
                    gate_ref, last_ref, prev_scr, *, c, tiles_per_seq):
    if tiles_per_seq:
        @pl.when(pl.program_id(0) == 0)
        def _():
            prev_scr[...] = jnp.zeros_like(prev_scr)

    h = _rms(x_ref[...], gpre_ref[...]).astype(BF16)
    wr = 3 * c + D_DECAY_LORA + D_ICLR_LORA + D_GATE_LORA
    p = jnp.dot(h, w_ref[:, :wr], preferred_element_type=F32)
    u = jnp.dot(h, w_ref[:, wr:wr + 2 * c], preferred_element_type=F32) + gb_ref[...]
    glu_ref[...] = u[:, :c] * _sigmoid(u[:, c:])
    gate_ref[...] = _sigmoid(jnp.dot(h, w_ref[:, wr + 2 * c:], preferred_element_type=F32) + mb_ref[...])
    if tiles_per_seq == 0:
        pp = sh_ref[...]
        last_ref[...] = p
    else:
        tm = p.shape[0]
        first = (pl.program_id(0) % tiles_per_seq) == 0
        prev_row = jnp.where(first, sh_ref[0], prev_scr[...])
        rows = lax.broadcasted_iota(jnp.int32, p.shape, 0)
        pp = jnp.where(rows == 0, prev_row, pltpu.roll(p, 1, axis=0))
        prev_scr[...] = p[tm - 1:tm, :]
        last_ref[0] = p[tm - 1:tm, :]
    xs = p + (pp - p) * mu_ref[...]
    r = xs[:, 0:c]
    k = xs[:, c:2 * c]
    v = xs[:, 2 * c:3 * c]
    wa = xs[:, 3 * c:3 * c + D_DECAY_LORA + D_ICLR_LORA]
    gd = xs[:, 3 * c + D_DECAY_LORA + D_ICLR_LORA:]
    z = w0_ref[...] + _dot(jnp.tanh(wa), w2_ref[...])
    lw_ref[...] = -math.exp(-0.5) * _sigmoid(z)
    a = _sigmoid(a0_ref[...] + _dot(wa, a2_ref[...]))
    g_ref[...] = _dot(_sigmoid(gd), g2_ref[...])
    kk = k * kk_ref[...]
    ss = _segsum(kk * kk, bd_ref[...])
    kk = kk * lax.rsqrt(jnp.maximum(ss, 1e-24))
    r_ref[...] = r
    k_ref[...] = k * (1.0 + (a - 1.0) * ka_ref[...])
    v_ref[...] = v
    a_ref[...] = -kk
    b_ref[...] = kk * a


def proj_in(x, shift0, seq_len, lpk, tm):
    m, d = x.shape
    c = lpk["c_rwkv"]
    wr = shift0.shape[-1]
    nseq = m // seq_len
    tm = _row_tile(m, tm)
    row = lambda n: pl.BlockSpec((tm, n), lambda i: (i, 0))
    if seq_len == 1:
        tiles_per_seq, sh, sh_spec = 0, shift0, row(wr)
        last_spec, last_shape = row(wr), (m, wr)
    else:
        assert seq_len % tm == 0
        tiles_per_seq = seq_len // tm
        sh = shift0.reshape(nseq, 1, wr)
        sh_spec = pl.BlockSpec((1, 1, wr), lambda i: (i // tiles_per_seq, 0, 0))
        last_spec, last_shape = sh_spec, (nseq, 1, wr)
    names = ("g_mix_pre", "w_in", "tshift_mu", "decay_w0", "decay_w2p", "iclr_a0", "iclr_a2p", "gate_g2", "k_k",
             "k_a", "bd", "glu_b", "merge_gate_b")
    params = [lpk[n] for n in names]
    outs = pl.pallas_call(
        functools.partial(_proj_in_kernel, c=c, tiles_per_seq=tiles_per_seq),
        grid=(m // tm,),
        in_specs=[row(d), sh_spec] + [_full(a.shape) for a in params],
        out_specs=[row(c)] * 8 + [row(2 * d), last_spec],
        out_shape=[jax.ShapeDtypeStruct((m, c), F32)] * 8 + [jax.ShapeDtypeStruct((m, 2 * d), F32),
                                                             jax.ShapeDtypeStruct(last_shape, F32)],
        scratch_shapes=[pltpu.VMEM((1, wr), F32)],
        compiler_params=_cparams("arbitrary"),
        name="proj_in",
    )(x, sh, *params)
    return outs[:9], outs[9].reshape(nseq, wr)


def _wkv_chunk_kernel(r_ref, lw_ref, k_ref, v_ref, a_ref, b_ref, y_ref, s_ref, g_scr, *, L, npairs, nb):
    ci = pl.program_id(1)

    @pl.when(ci == 0)
    def _():
        g_scr[...] = jnp.zeros_like(g_scr)

    row = lax.broadcasted_iota(jnp.int32, (L, L), 0)
    col = lax.broadcasted_iota(jnp.int32, (L, L), 1)
    ltri = jnp.where(col <= row, 1.0, 0.0).astype(BF16)
    d = functools.partial(jnp.dot, preferred_element_type=F32)
    m0 = lax.broadcasted_iota(jnp.int32, (L, LANES), 1) < HEAD_SIZE
    i2 = lax.broadcasted_iota(jnp.int32, (2 * L, 2 * L), 0)
    j2 = lax.broadcasted_iota(jnp.int32, (2 * L, 2 * L), 1)
    same = (i2 < L) == (j2 < L)
    tril_s = same & (j2 < i2)
    tril_i = same & (j2 <= i2)
    eye = jnp.where(i2 == j2, 1.0, 0.0)

    def stack(x, p):
        xp = x[:, p * LANES:(p + 1) * LANES]
        return jnp.concatenate([jnp.where(m0, xp, 0.0), jnp.where(m0, 0.0, xp)], axis=0)

    inst = [(j, p) for j in range(nb) for p in range(npairs)]
    lhs, rhs, vm, p_last = [], [], [], []
    for j in range(nb):
        lw = lw_ref[j]
        hi, mid, lo = _split3(lw)
        cs = d(ltri, hi) + (d(ltri, mid) + d(ltri, lo))
        pin = jnp.exp(cs)
        pinv = jnp.exp(-cs)
        at = a_ref[j] * jnp.exp(cs - lw)
        rt = r_ref[j] * pin
        bt = b_ref[j] * pinv
        kt = k_ref[j] * pinv
        v = v_ref[j]
        for p in range(npairs):
            lhs.append(jnp.concatenate([stack(at, p), stack(rt, p)], axis=0).astype(BF16))
            rhs.append(jnp.concatenate([stack(bt, p), stack(kt, p)], axis=0))
            vm.append(stack(v, p).astype(BF16))
            p_last.append(pin[L - 1:L, p * LANES:(p + 1) * LANES])
    n = len(inst)
    gm = [_dot_nt(lhs[i], rhs[i]) for i in range(n)]
    a_ab = [jnp.where(tril_s, gm[i][:2 * L, :2 * L], 0.0) for i in range(n)]
    a_kr = [jnp.concatenate([jnp.where(tril_s, gm[i][:2 * L, 2 * L:], 0.0),
                             jnp.where(tril_i, gm[i][2 * L:, 2 * L:], 0.0)], axis=0) for i in range(n)]
    a_rb = [jnp.where(tril_i, gm[i][2 * L:, :2 * L], 0.0) for i in range(n)]
    rhs_t = [rhs[i].T.astype(BF16) for i in range(n)]
    decay = [jnp.broadcast_to(p_last[i], (LANES, LANES)).T for i in range(n)]
    lhs2 = [jnp.concatenate([lhs[i], a_kr[i].astype(BF16)], axis=1) for i in range(n)]
    t = [eye + a_ab[i] for i in range(n)]
    ak = [_dot(a_ab[i], a_ab[i]) for i in range(n)]
    for _ in range(int(math.log2(L)) - 2):
        pr = [_dot(ak[i], jnp.concatenate([ak[i], t[i]], axis=1)) for i in range(n)]
        ak = [pr[i][:, :2 * L] for i in range(n)]
        t = [t[i] + pr[i][:, 2 * L:] for i in range(n)]
    t = [t[i] + _dot(ak[i], t[i]) for i in range(n)]
    tc = [jnp.concatenate([t[i], _dot(a_rb[i], t[i])], axis=0) for i in range(n)]
    h = [g_scr[j, p] for (j, p) in inst]
    xy = [_dot(lhs2[i], jnp.concatenate([h[i].astype(BF16), vm[i]], axis=0)) for i in range(n)]
    uy = [_dot(tc[i], xy[i][:2 * L]) for i in range(n)]
    for i, (j, p) in enumerate(inst):
        yy = xy[i][2 * L:] + uy[i][2 * L:]
        y_ref[j, :, p * LANES:(p + 1) * LANES] = yy[:L] + yy[L:]
        uv = jnp.concatenate([uy[i][:2 * L].astype(BF16), vm[i]], axis=0)
        g_scr[j, p] = (h[i] + _dot(rhs_t[i], uv)) * decay[i]

    @pl.when(ci == pl.num_programs(1) - 1)
    def _():
        for j, p in inst:
            s_ref[j, p] = g_scr[j, p].T


def wkv_chunked(r, lw, k, v, a, b, nb=1):
    bsz, t, c = r.shape
    L = WKV_CHUNK
    assert t % L == 0 and c % LANES == 0 and bsz % nb == 0
    npairs = c // LANES
    seq = pl.BlockSpec((nb, L, c), lambda bi, ci: (bi, ci, 0))
    y, s = pl.pallas_call(
        functools.partial(_wkv_chunk_kernel, L=L, npairs=npairs, nb=nb),
        grid=(bsz // nb, t // L),
        in_specs=[seq] * 6,
        out_specs=[seq, pl.BlockSpec((nb, npairs, LANES, LANES), lambda bi, ci: (bi, 0, 0, 0))],
        out_shape=[jax.ShapeDtypeStruct((bsz, t, c), F32),
                   jax.ShapeDtypeStruct((bsz, npairs, LANES, LANES), F32)],
        scratch_shapes=[pltpu.VMEM((nb, npairs, LANES, LANES), F32)],
        compiler_params=_cparams("parallel", "arbitrary"),
        name="wkv_chunked",
    )(r, lw, k, v, a, b)
    n = HEAD_SIZE
    s = jnp.stack([s[:, :, :n, :n], s[:, :, n:, n:]], axis=2)
    return y, s.reshape(bsz, 2 * npairs, n, n)


def _wkv_step_kernel(*refs, chained):
    s_ref, r_ref, lw_ref, k_ref, v_ref, a_ref, b_ref, so_ref, y_ref = refs[1:] if chained else refs
    s = s_ref[...]
    sa = jnp.sum(s * a_ref[0][None], axis=1)
    s = s * jnp.exp(lw_ref[0])[None] + sa[:, None, :] * b_ref[0][None] + v_ref[0][:, None, :] * k_ref[0][None]
    so_ref[...] = s
    y_ref[0] = jnp.sum(s * r_ref[0][None], axis=1)


def _chain_out(prev, shape):
    if prev is None:
        return [], [], {}
    assert prev.shape == tuple(shape)
    return [pl.BlockSpec(memory_space=pl.ANY)], [prev], {0: 0}


def wkv_step(s_in, s_out, layer, r, lw, k, v, a, b):
    _, nh, n, _, bsz = s_in.shape
    vec = pl.BlockSpec((1, n, bsz), lambda h: (h, 0, 0))
    st = pl.BlockSpec((None, None, n, n, bsz), lambda h: (layer, h, 0, 0, 0))
    hv = lambda x: x.T.reshape(nh, n, bsz)
    c_specs, c_ops, c_alias = _chain_out(s_out, s_in.shape)
    s_out, y = pl.pallas_call(
        functools.partial(_wkv_step_kernel, chained=bool(c_ops)),
        grid=(nh,),
        in_specs=c_specs + [st] + [vec] * 6,
        out_specs=[st, vec],
        out_shape=[jax.ShapeDtypeStruct(s_in.shape, F32), jax.ShapeDtypeStruct((nh, n, bsz), F32)],
        input_output_aliases=c_alias,
        compiler_params=_cparams("parallel"),
        name="wkv_step",
    )(*c_ops, s_in, hv(r), hv(lw), hv(k), hv(v), hv(a), hv(b))
    return y.reshape(nh * n, bsz).T, s_out


CONV_PAD = 32


CONV_ROWS = 32


def _ln_swish(y, g_ref, b_ref):
    mu = jnp.mean(y, axis=-1, keepdims=True)
    dlt = y - mu
    var = jnp.mean(dlt * dlt, axis=-1, keepdims=True)
    y = dlt * lax.rsqrt(var + LN_EPS) * g_ref[...] + b_ref[...]
    return y * _sigmoid(y)


def _conv_seq_kernel(u_ref, buf_ref, cw_ref, cb_ref, lg_ref, lb_ref, o_ref, nb_ref, xp_scr, *, tt):
    ti = pl.program_id(1)

    @pl.when(ti == 0)
    def _():
        xp_scr[0:CONV_PAD, :] = buf_ref[0]

    xp_scr[CONV_PAD:CONV_PAD + tt, :] = u_ref[0]
    base = CONV_PAD - (CONV_WIDTH - 1)
    rb = CONV_ROWS

    for i in range(tt // rb):
        s0 = i * rb
        acc = None
        for r in range(SUBLANES):
            rows = rb if r == 0 else rb + SUBLANES
            part = None
            for j in range((CONV_PAD + SUBLANES) // SUBLANES):
                w = SUBLANES * j + r - base
                if 0 <= w < CONV_WIDTH:
                    term = xp_scr[pl.ds(s0 + SUBLANES * j, rows), :] * cw_ref[w:w + 1, :]
                    part = term if part is None else part + term
            part = part if r == 0 else part[r:r + rb]
            acc = part if acc is None else acc + part
        o_ref[0, pl.ds(s0, rb), :] = _ln_swish(acc + cb_ref[...], lg_ref, lb_ref)
    tail = xp_scr[tt:tt + CONV_PAD, :]
    xp_scr[0:CONV_PAD, :] = tail

    @pl.when(ti == pl.num_programs(1) - 1)
    def _():
        nb_ref[0] = tail


def conv_seq(u, buf0, lpk, tt):
    bsz, t, c = u.shape
    tt = _row_tile(t, tt)
    assert tt % CONV_ROWS == 0
    pad = CONV_PAD - (CONV_WIDTH - 1)
    bufp = jnp.pad(buf0, ((0, 0), (pad, 0), (0, 0)))
    params = [lpk[n] for n in ("conv_w", "conv_b", "conv_ln_g", "conv_ln_b")]
    hist = pl.BlockSpec((1, CONV_PAD, c), lambda bi, ti: (bi, 0, 0))
    out, nb = pl.pallas_call(
        functools.partial(_conv_seq_kernel, tt=tt),
        grid=(bsz, t // tt),
        in_specs=[pl.BlockSpec((1, tt, c), lambda bi, ti: (bi, ti, 0)), hist] + [_full(a.shape) for a in params],
        out_specs=[pl.BlockSpec((1, tt, c), lambda bi, ti: (bi, ti, 0)), hist],
        out_shape=[jax.ShapeDtypeStruct((bsz, t, c), F32), jax.ShapeDtypeStruct((bsz, CONV_PAD, c), F32)],
        scratch_shapes=[pltpu.VMEM((CONV_PAD + tt, c), F32)],
        compiler_params=_cparams("parallel", "arbitrary"),
        name="conv_seq",
    )(u, bufp, *params)
    return out, nb[:, pad:]


def _conv_step_kernel(*refs, chained):
    u_ref, buf_ref, cw_ref, cb_ref, lg_ref, lb_ref, nb_ref, o_ref = refs[1:] if chained else refs
    hist = CONV_WIDTH - 1
    glu = u_ref[...]
    acc = glu * cw_ref[hist:hist + 1, :] + cb_ref[...]
    for t in range(hist):
        slab = buf_ref[t]
        acc = acc + slab * cw_ref[t:t + 1, :]
        if t > 0:
            nb_ref[t - 1] = slab
    nb_ref[hist - 1] = glu
    o_ref[...] = _ln_swish(acc, lg_ref, lb_ref)


def conv_step(u, buf_in, buf_out, layer, lpk):
    bsz, c = u.shape
    params = [lpk[n] for n in ("conv_w", "conv_b", "conv_ln_g", "conv_ln_b")]
    hist = pl.BlockSpec((None, CONV_WIDTH - 1, bsz, c), lambda i: (layer, 0, 0, 0))
    c_specs, c_ops, c_alias = _chain_out(buf_out, buf_in.shape)
    buf_out, out = pl.pallas_call(
        functools.partial(_conv_step_kernel, chained=bool(c_ops)),
        grid=(1,),
        in_specs=c_specs + [_full((bsz, c)), hist] + [_full(a.shape) for a in params],
        out_specs=[hist, _full((bsz, c))],
        out_shape=[jax.ShapeDtypeStruct(buf_in.shape, F32), jax.ShapeDtypeStruct((bsz, c), F32)],
        input_output_aliases=c_alias,
        compiler_params=_cparams("parallel"),
        name="conv_step",
    )(*c_ops, u, buf_in, *params)
    return out, buf_out


def _merge_kernel(x_ref, y_ref, r_ref, k_ref, v_ref, g_ref, cv_ref, gate_ref, bd_ref, lg_ref, lb_ref, rk_ref,
                  wa_ref, wp_ref, bp_ref, wo_ref, gp_ref, o_ref, *, d):
    bd = bd_ref[...]
    inv_n = 1.0 / HEAD_SIZE
    y = y_ref[...]
    mu = _segsum(y, bd) * inv_n
    dlt = y - mu
    var = _segsum(dlt * dlt, bd) * inv_n
    yn = dlt * lax.rsqrt(var + GN_EPS) * lg_ref[...] + lb_ref[...]
    v = v_ref[...]
    bonus = _segsum(r_ref[...] * k_ref[...] * rk_ref[...], bd) * v
    za = (yn + bonus) * g_ref[...]
    ya = _dot(za, wa_ref[...])
    yb = _dot(cv_ref[...], wp_ref[...]) + bp_ref[...]
    mixed = _dot(gate_ref[:, :d] * ya + gate_ref[:, d:] * yb, wo_ref[...])
    o_ref[...] = x_ref[...] + _rms(mixed, gp_ref[...])


def merge(x, y, r, k, v, g, cv, gate, lpk, tm):
    m, d = x.shape
    c = y.shape[1]
    tm = _row_tile(m, tm)
    row = lambda n: pl.BlockSpec((tm, n), lambda i: (i, 0))
    names = ("bd", "lnx_g", "lnx_b", "r_k", "w_out_rwkv", "w_pw", "b_pw", "w_out", "g_mix_post")
    params = [lpk[n] for n in names]
    return pl.pallas_call(
        functools.partial(_merge_kernel, d=d),
        grid=(m // tm,),
        in_specs=[row(d)] + [row(c)] * 6 + [row(2 * d)] + [_full(a.shape) for a in params],
        out_specs=row(d),
        out_shape=jax.ShapeDtypeStruct((m, d), F32),
        compiler_params=_cparams("parallel"),
        name="merge",
    )(x, y, r, k, v, g, cv, gate, *params)


def _xattn_kernel(x_ref, mk_ref, mv_ref, gq_ref, wq_ref, wo_ref, gp_ref, o_ref, mk_scr, mv_scr, *, dh):
    @pl.when(pl.program_id(1) == 0)
    def _():
        mk_scr[...] = mk_ref[0].astype(BF16)
        mv_scr[...] = mv_ref[0].astype(BF16)

    x = x_ref[0]
    q = _dot(_rms(x, gq_ref[...]), wq_ref[...])
    scale = dh ** -0.5
    outs = []
    for h in range(N_XHEADS):
        sl = slice(h * dh, (h + 1) * dh)
        s = _dot_nt(q[:, sl], mk_scr[:, sl]) * scale
        s = s - jnp.max(s, axis=-1, keepdims=True)
        e = jnp.exp(s)
        pr = e / jnp.sum(e, axis=-1, keepdims=True)
        outs.append(_dot(pr, mv_scr[:, sl]))
    o = jnp.concatenate(outs, axis=1)
    o_ref[0] = x + _rms(_dot(o, wo_ref[...]), gp_ref[...])


def xattn(x, mk, mv, lpk, tq):
    bsz, t, d = x.shape
    nm = mk.shape[1]
    tq = _row_tile(t, tq)
    params = [lpk[n] for n in ("g_x_pre", "w_q", "w_o_x", "g_x_post")]
    xs = pl.BlockSpec((1, tq, d), lambda bi, ti: (bi, ti, 0))
    ms = pl.BlockSpec((1, nm, d), lambda bi, ti: (bi, 0, 0))
    return pl.pallas_call(
        functools.partial(_xattn_kernel, dh=d // N_XHEADS),
        grid=(bsz, t // tq),
        in_specs=[xs, ms, ms] + [_full(a.shape) for a in params],
        out_specs=xs,
        out_shape=jax.ShapeDtypeStruct(x.shape, F32),
        scratch_shapes=[pltpu.VMEM((nm, d), BF16), pltpu.VMEM((nm, d), BF16)],
        compiler_params=_cparams("parallel", "arbitrary"),
        name="xattn",
    )(x, mk, mv, *params)


def _xattn_step_kernel(q_ref, k_ref, v_ref, o_ref, *, bb, nm, nh, dh):
    assert 2 * nh == SUBLANES
    g = nm * nh // SUBLANES
    for j in range(bb):
        q = q_ref[j]
        q8 = jnp.concatenate([q, q], axis=0)
        kf = k_ref[j].reshape(nm * nh, dh).reshape(g, SUBLANES, dh)
        vf = v_ref[j].reshape(nm * nh, dh).reshape(g, SUBLANES, dh)
        s = jnp.sum(kf * q8[None], axis=-1, keepdims=True) * dh ** -0.5
        mx = jnp.max(s, axis=0, keepdims=True)
        mx = jnp.maximum(mx, pltpu.roll(mx[0], nh, axis=0)[None])
        e = jnp.exp(s - mx)
        den = jnp.sum(e, axis=0, keepdims=True)
        den = den + pltpu.roll(den[0], nh, axis=0)[None]
        o8 = jnp.sum((e / den) * vf, axis=0)
        o_ref[j] = o8[:nh] + o8[nh:]


def xattn_step(q, cache_k, cache_v, layer):
    bsz, d = q.shape
    _, _, nm, nh, dh = cache_k.shape
    bb = _row_tile(bsz, 2)
    qs = pl.BlockSpec((bb, nh, dh), lambda i: (i, 0, 0))
    cs = pl.BlockSpec((None, bb, nm, nh, dh), lambda i: (layer, i, 0, 0, 0))
    o = pl.pallas_call(
        functools.partial(_xattn_step_kernel, bb=bb, nm=nm, nh=nh, dh=dh),
        grid=(bsz // bb,),
        in_specs=[qs, cs, cs],
        out_specs=qs,
        out_shape=jax.ShapeDtypeStruct((bsz, nh, dh), F32),
        compiler_params=_cparams("parallel"),
        name="xattn_step",
    )(q.reshape(bsz, nh, dh), cache_k, cache_v)
    return o.reshape(bsz, d)


def _mm_norm_res_kernel(a_ref, x_ref, w_ref, g_ref, o_ref):
    o_ref[...] = x_ref[...] + _rms(_dot(a_ref[...], w_ref[...]), g_ref[...])


def mm_norm_res(a, x, w, g, tm):
    m, k = a.shape
    n = w.shape[1]
    tm = _row_tile(m, tm)
    return pl.pallas_call(
        _mm_norm_res_kernel,
        grid=(m // tm,),
        in_specs=[pl.BlockSpec((tm, k), lambda i: (i, 0)), pl.BlockSpec((tm, n), lambda i: (i, 0)),
                  _full((k, n)), _full((1, n))],
        out_specs=pl.BlockSpec((tm, n), lambda i: (i, 0)),
        out_shape=jax.ShapeDtypeStruct((m, n), F32),
        compiler_params=_cparams("parallel"),
        name="mm_norm_res",
    )(a, x, w, g)


MXU_K = 256


def _ffn_chunks(dff):
    half = -(-(dff // 2) // MXU_K) * MXU_K
    return ((0, half), (half, dff)) if half < dff else ((0, dff),)


def _ffn_kernel(x_ref, gpre_ref, wgu_ref, wd_ref, gpost_ref, o_ref, *, dff):
    x = x_ref[...]
    h = _rms(x, gpre_ref[...]).astype(BF16)
    acc = None
    for lo, hi in _ffn_chunks(dff):
        gt = jnp.dot(h, wgu_ref[:, lo:hi], preferred_element_type=F32)
        up = jnp.dot(h, wgu_ref[:, dff + lo:dff + hi], preferred_element_type=F32)
        part = _dot(gt * _sigmoid(gt) * up, wd_ref[lo:hi, :])
        acc = part if acc is None else acc + part
    o_ref[...] = x + _rms(acc, gpost_ref[...])


def _resident(shape):
    nd = len(shape)
    return pl.BlockSpec(shape, lambda *_: (0,) * nd, pipeline_mode=pl.Buffered(1))


def ffn(x, lpk, tm):
    m, d = x.shape
    w_gu, w_dn = lpk["w_gate_up"], lpk["w_down"]
    dff = w_dn.shape[0]
    assert dff % LANES == 0
    tm = _row_tile(m, tm)
    return pl.pallas_call(
        functools.partial(_ffn_kernel, dff=dff),
        grid=(m // tm,),
        in_specs=[pl.BlockSpec((tm, d), lambda i: (i, 0)), _full((1, d)), _resident(w_gu.shape),
                  _resident(w_dn.shape), _full((1, d))],
        out_specs=pl.BlockSpec((tm, d), lambda i: (i, 0)),
        out_shape=jax.ShapeDtypeStruct((m, d), F32),
        compiler_params=_cparams("parallel"),
        name="ffn",
    )(x, lpk["g_ffn_pre"], w_gu, w_dn, lpk["g_ffn_post"])


def _layer_params(l, w):
    row = lambda a: a[l].reshape(1, -1)
    c = w["w_out_rwkv"].shape[1]
    lora_rows = D_DECAY_LORA + D_ICLR_LORA
    w2p = jnp.zeros((lora_rows, c), F32).at[:D_DECAY_LORA].set(w["decay_w2"][l])
    a2p = jnp.zeros((lora_rows, c), F32).at[D_DECAY_LORA:].set(w["iclr_a2"][l])
    head = jnp.arange(c) // HEAD_SIZE
    lpk = {
        "c_rwkv": c,
        "bd": (head[:, None] == head[None, :]).astype(BF16),
        "decay_w2p": w2p.astype(BF16), "iclr_a2p": a2p.astype(BF16),
        "conv_w": jnp.pad(w["conv_w"][l], ((0, CONV_PAD - CONV_WIDTH), (0, 0))),
    }
    for n in ("w_in", "gate_g2", "w_out_rwkv", "w_pw", "w_out", "w_q", "w_o_x", "w_gate_up", "w_down"):
        lpk[n] = w[n][l].astype(BF16)
    lpk["w_kv"] = jnp.concatenate([w["w_k"][l], w["w_v"][l]], axis=1).astype(BF16)
    for n in ("tshift_mu", "decay_w0", "iclr_a0", "k_k", "k_a", "r_k", "lnx_g", "lnx_b", "glu_b", "conv_b",
              "conv_ln_g", "conv_ln_b", "b_pw", "merge_gate_b", "g_mix_pre", "g_mix_post", "g_mem", "g_x_pre",
              "g_x_post", "g_ffn_pre", "g_ffn_post"):
        lpk[n] = row(w[n])
    return lpk


PROMPT_TM = 256
SAMPLE_TM = 128


def _prompt_layer(x, mem_k, mem_v, shift0, buf0, lpk):
    bsz, t, d = x.shape
    m = bsz * t
    c = lpk["c_rwkv"]
    x2 = x.reshape(m, d)
    (r, lw, k, v, a, b, g, glu, gate), shift1 = proj_in(x2, shift0, t, lpk, PROMPT_TM)
    seq = lambda z: z.reshape(bsz, t, c)
    y, s1 = wkv_chunked(seq(r), seq(lw), seq(k), seq(v), seq(a), seq(b), nb=math.gcd(bsz, WKV_SEQS))
    cv, buf1 = conv_seq(seq(glu), buf0, lpk, 256)
    x2 = merge(x2, y.reshape(m, c), r, k, v, g, cv.reshape(m, c), gate, lpk, PROMPT_TM)
    x2 = xattn(x2.reshape(bsz, t, d), mem_k, mem_v, lpk, 512 if t % 512 == 0 else 256).reshape(m, d)
    x2 = ffn(x2, lpk, 2 * PROMPT_TM)
    return x2.reshape(bsz, t, d), s1, shift1, buf1


def _sample_layer(x2, cache_k, cache_v, s_in, s_out, shift0, buf_in, buf_out, layer, lpk):
    d = x2.shape[1]
    (r, lw, k, v, a, b, g, glu, gate), shift1 = proj_in(x2, shift0, 1, lpk, SAMPLE_TM)
    y, s_out = wkv_step(s_in, s_out, layer, r, lw, k, v, a, b)
    cv, buf_out = conv_step(glu, buf_in, buf_out, layer, lpk)
    x2 = merge(x2, y, r, k, v, g, cv, gate, lpk, SAMPLE_TM)
    (q,) = norm_mm(x2, lpk["g_x_pre"], lpk["w_q"], (d,), SAMPLE_TM)
    o = xattn_step(q, cache_k, cache_v, layer)
    x2 = mm_norm_res(o, x2, lpk["w_o_x"], lpk["g_x_post"], SAMPLE_TM)
    x2 = ffn(x2, lpk, SAMPLE_TM)
    return x2, s_out, shift1, buf_out


def kernel(x_prompt, x_sample, mem_prompt, state_rwkv, state_shift, state_conv, cache_mem_k, cache_mem_v, w_in, tshift_mu, decay_w0, decay_w2, iclr_a0, iclr_a2, gate_g2, k_k, k_a, r_k, lnx_g, lnx_b, w_out_rwkv, glu_b, conv_w, conv_b, conv_ln_g, conv_ln_b, w_pw, b_pw, merge_gate_b, w_out, g_mix_pre, g_mix_post, g_mem, w_q, w_k, w_v, w_o_x, g_x_pre, g_x_post, g_ffn_pre, g_ffn_post, w_gate_up, w_down):
    w = dict(w_in=w_in, tshift_mu=tshift_mu, decay_w0=decay_w0, decay_w2=decay_w2, iclr_a0=iclr_a0,
             iclr_a2=iclr_a2, gate_g2=gate_g2, k_k=k_k, k_a=k_a, r_k=r_k, lnx_g=lnx_g, lnx_b=lnx_b,
             w_out_rwkv=w_out_rwkv, glu_b=glu_b, conv_w=conv_w, conv_b=conv_b, conv_ln_g=conv_ln_g,
             conv_ln_b=conv_ln_b, w_pw=w_pw, b_pw=b_pw, merge_gate_b=merge_gate_b, w_out=w_out,
             g_mix_pre=g_mix_pre, g_mix_post=g_mix_post, g_mem=g_mem, w_q=w_q, w_k=w_k, w_v=w_v, w_o_x=w_o_x,
             g_x_pre=g_x_pre, g_x_post=g_x_post, g_ffn_pre=g_ffn_pre, g_ffn_post=g_ffn_post,
             w_gate_up=w_gate_up, w_down=w_down)
    depth = w_in.shape[0]
    bp, _, d = x_prompt.shape
    bs = x_sample.shape[0]
    nm = mem_prompt.shape[1]
    c = w_out_rwkv.shape[1]
    wr = state_shift.shape[-1]
    xp, xs = x_prompt, x_sample.reshape(bs, d)
    mem2 = mem_prompt.reshape(bp * nm, d)
    s_all, buf_all = None, None
    state_t = jnp.transpose(state_rwkv, (0, 2, 3, 4, 1))
    conv_t = jnp.transpose(state_conv, (0, 2, 1, 3))
    outs = {n: [] for n in ("rwkv_p", "shift_p", "conv_p", "memk_p", "memv_p", "shift_s")}
    dh = d // N_XHEADS
    for l in range(depth):
        lpk = _layer_params(l, w)
        mk, mv = norm_mm(mem2, lpk["g_mem"], lpk["w_kv"], (d, d), PROMPT_TM)
        mk, mv = mk.reshape(bp, nm, d), mv.reshape(bp, nm, d)
        xp, s_p, sh_p, cb_p = _prompt_layer(xp, mk, mv, jnp.zeros((bp, wr), F32),
                                            jnp.zeros((bp, CONV_WIDTH - 1, c), F32), lpk)
        xs, s_all, sh_s, buf_all = _sample_layer(xs, cache_mem_k, cache_mem_v, state_t, s_all, state_shift[l],
                                                 conv_t, buf_all, l, lpk)
        outs["rwkv_p"].append(s_p); outs["shift_p"].append(sh_p); outs["conv_p"].append(cb_p)
        outs["memk_p"].append(mk.reshape(bp, nm, N_XHEADS, dh)); outs["memv_p"].append(mv.reshape(bp, nm, N_XHEADS, dh))
        outs["shift_s"].append(sh_s)
    st = {n: jnp.stack(v) for n, v in outs.items()}
    return (xp, xs.reshape(bs, 1, d), st["rwkv_p"], st["shift_p"], st["conv_p"], st["memk_p"], st["memv_p"],
            jnp.transpose(s_all, (0, 4, 1, 2, 3)), st["shift_s"], jnp.transpose(buf_all, (0, 2, 1, 3)))
```

```python
import functools
import math

import jax
import jax.numpy as jnp
from jax import lax
from jax.experimental import pallas as pl
from jax.experimental.pallas import tpu as pltpu

F32 = jnp.float32
BF16 = jnp.bfloat16

HEAD_SIZE = 64
D_DECAY_LORA = 64
D_ICLR_LORA = 64
D_GATE_LORA = 128
CONV_WIDTH = 31
N_XHEADS = 4
GN_EPS = 64e-5
NORM_EPS = 1e-6
LN_EPS = 1e-5

LANES = 128
SUBLANES = 8
VMEM_LIMIT_BYTES = 56 * 1024 * 1024

WKV_CHUNK = 64
WKV_SEQS = 4


def _cparams(*sem):
    return pltpu.CompilerParams(dimension_semantics=sem, vmem_limit_bytes=VMEM_LIMIT_BYTES)


def _dot(a, b):
    return jnp.dot(a.astype(BF16), b.astype(BF16), preferred_element_type=F32)


def _dot_nt(a, b):
    return lax.dot_general(a.astype(BF16), b.astype(BF16), (((1,), (1,)), ((), ())),
                           preferred_element_type=F32)


def _split3(x):
    hi = x.astype(BF16)
    r1 = x - hi.astype(F32)
    mid = r1.astype(BF16)
    lo = (r1 - mid.astype(F32)).astype(BF16)
    return hi, mid, lo


def _segsum(x, bd):
    return jnp.dot(x.astype(BF16), bd, preferred_element_type=F32)


def _rms(x, g):
    return x * lax.rsqrt(jnp.mean(x * x, axis=-1, keepdims=True) + NORM_EPS) * g


def _sigmoid(x):
    return 0.5 * jnp.tanh(0.5 * x) + 0.5


def _row_tile(m, target):
    t = min(m, target)
    assert m % t == 0, (m, t)
    return t


def _full(shape):
    nd = len(shape)
    return pl.BlockSpec(shape, lambda *_: (0,) * nd)


def _norm_mm_kernel(x_ref, g_ref, w_ref, *o_refs, splits):
    h = _rms(x_ref[...], g_ref[...]).astype(BF16)
    off = 0
    for o_ref, n in zip(o_refs, splits):
        o_ref[...] = jnp.dot(h, w_ref[:, off:off + n], preferred_element_type=F32)
        off += n


def norm_mm(x, g, w, splits, tm):
    m, k = x.shape
    tm = _row_tile(m, tm)
    n_all = w.shape[1]
    assert sum(splits) == n_all and all(s % LANES == 0 for s in splits)
    return pl.pallas_call(
        functools.partial(_norm_mm_kernel, splits=tuple(splits)),
        grid=(m // tm,),
        in_specs=[pl.BlockSpec((tm, k), lambda i: (i, 0)), _full((1, k)), _full((k, n_all))],
        out_specs=[pl.BlockSpec((tm, n), lambda i: (i, 0)) for n in splits],
        out_shape=[jax.ShapeDtypeStruct((m, n), F32) for n in splits],
        compiler_params=_cparams("parallel"),
        name="norm_mm",
    )(x, g.reshape(1, k), w)


def _proj_in_kernel(x_ref, sh_ref, gpre_ref, w_ref, mu_ref, w0_ref, w2_ref, a0_ref, a2_ref, g2_ref, kk_ref,
                    ka_ref, bd_ref, gb_ref, mb_ref, r_ref, lw_ref, k_ref, v_ref, a_ref, b_ref, g_ref, glu_ref,
                    gate_ref, last_ref, prev_scr, *, c, tiles_per_seq):
    if tiles_per_seq:
        @pl.when(pl.program_id(0) == 0)
        def _():
            prev_scr[...] = jnp.zeros_like(prev_scr)

    h = _rms(x_ref[...], gpre_ref[...]).astype(BF16)
    wr = 3 * c + D_DECAY_LORA + D_ICLR_LORA + D_GATE_LORA
    p = jnp.dot(h, w_ref[:, :wr], preferred_element_type=F32)
    u = jnp.dot(h, w_ref[:, wr:wr + 2 * c], preferred_element_type=F32) + gb_ref[...]
    glu_ref[...] = u[:, :c] * _sigmoid(u[:, c:])
    gate_ref[...] = _sigmoid(jnp.dot(h, w_ref[:, wr + 2 * c:], preferred_element_type=F32) + mb_ref[...])
    if tiles_per_seq == 0:
        pp = sh_ref[...]
        last_ref[...] = p
    else:
        tm = p.shape[0]
        first = (pl.program_id(0) % tiles_per_seq) == 0
        prev_row = jnp.where(first, sh_ref[0], prev_scr[...])
        rows = lax.broadcasted_iota(jnp.int32, p.shape, 0)
        pp = jnp.where(rows == 0, prev_row, pltpu.roll(p, 1, axis=0))
        prev_scr[...] = p[tm - 1:tm, :]
        last_ref[0] = p[tm - 1:tm, :]
    xs = p + (pp - p) * mu_ref[...]
    r = xs[:, 0:c]
    k = xs[:, c:2 * c]
    v = xs[:, 2 * c:3 * c]
    wa = xs[:, 3 * c:3 * c + D_DECAY_LORA + D_ICLR_LORA]
    gd = xs[:, 3 * c + D_DECAY_LORA + D_ICLR_LORA:]
    z = w0_ref[...] + _dot(jnp.tanh(wa), w2_ref[...])
    lw_ref[...] = -math.exp(-0.5) * _sigmoid(z)
    a = _sigmoid(a0_ref[...] + _dot(wa, a2_ref[...]))
    g_ref[...] = _dot(_sigmoid(gd), g2_ref[...])
    kk = k * kk_ref[...]
    ss = _segsum(kk * kk, bd_ref[...])
    kk = kk * lax.rsqrt(jnp.maximum(ss, 1e-24))
    r_ref[...] = r
    k_ref[...] = k * (1.0 + (a - 1.0) * ka_ref[...])
    v_ref[...] = v
    a_ref[...] = -kk
    b_ref[...] = kk * a


def proj_in(x, shift0, seq_len, lpk, tm):
    m, d = x.shape
    c = lpk["c_rwkv"]
    wr = shift0.shape[-1]
    nseq = m // seq_len
    tm = _row_tile(m, tm)
    row = lambda n: pl.BlockSpec((tm, n), lambda i: (i, 0))
    if seq_len == 1:
        tiles_per_seq, sh, sh_spec = 0, shift0, row(wr)
        last_spec, last_shape = row(wr), (m, wr)
    else:
        assert seq_len % tm == 0
        tiles_per_seq = seq_len // tm
        sh = shift0.reshape(nseq, 1, wr)
        sh_spec = pl.BlockSpec((1, 1, wr), lambda i: (i // tiles_per_seq, 0, 0))
        last_spec, last_shape = sh_spec, (nseq, 1, wr)
    names = ("g_mix_pre", "w_in", "tshift_mu", "decay_w0", "decay_w2p", "iclr_a0", "iclr_a2p", "gate_g2", "k_k",
             "k_a", "bd", "glu_b", "merge_gate_b")
    params = [lpk[n] for n in names]
    outs = pl.pallas_call(
        functools.partial(_proj_in_kernel, c=c, tiles_per_seq=tiles_per_seq),
        grid=(m // tm,),
        in_specs=[row(d), sh_spec] + [_full(a.shape) for a in params],
        out_specs=[row(c)] * 8 + [row(2 * d), last_spec],
        out_shape=[jax.ShapeDtypeStruct((m, c), F32)] * 8 + [jax.ShapeDtypeStruct((m, 2 * d), F32),
                                                             jax.ShapeDtypeStruct(last_shape, F32)],
        scratch_shapes=[pltpu.VMEM((1, wr), F32)],
        compiler_params=_cparams("arbitrary"),
        name="proj_in",
    )(x, sh, *params)
    return outs[:9], outs[9].reshape(nseq, wr)


def _wkv_chunk_kernel(r_ref, lw_ref, k_ref, v_ref, a_ref, b_ref, y_ref, s_ref, g_scr, *, L, npairs, nb):
    ci = pl.program_id(1)

    @pl.when(ci == 0)
    def _():
        g_scr[...] = jnp.zeros_like(g_scr)

    row = lax.broadcasted_iota(jnp.int32, (L, L), 0)
    col = lax.broadcasted_iota(jnp.int32, (L, L), 1)
    ltri = jnp.where(col <= row, 1.0, 0.0).astype(BF16)
    d = functools.partial(jnp.dot, preferred_element_type=F32)
    m0 = lax.broadcasted_iota(jnp.int32, (L, LANES), 1) < HEAD_SIZE
    i2 = lax.broadcasted_iota(jnp.int32, (2 * L, 2 * L), 0)
    j2 = lax.broadcasted_iota(jnp.int32, (2 * L, 2 * L), 1)
    same = (i2 < L) == (j2 < L)
    tril_s = same & (j2 < i2)
    tril_i = same & (j2 <= i2)
    eye = jnp.where(i2 == j2, 1.0, 0.0)

    def stack(x, p):
        xp = x[:, p * LANES:(p + 1) * LANES]
        return jnp.concatenate([jnp.where(m0, xp, 0.0), jnp.where(m0, 0.0, xp)], axis=0)

    inst = [(j, p) for j in range(nb) for p in range(npairs)]
    lhs, rhs, vm, p_last = [], [], [], []
    for j in range(nb):
        lw = lw_ref[j]
        hi, mid, lo = _split3(lw)
        cs = d(ltri, hi) + (d(ltri, mid) + d(ltri, lo))
        pin = jnp.exp(cs)
        pinv = jnp.exp(-cs)
        at = a_ref[j] * jnp.exp(cs - lw)
        rt = r_ref[j] * pin
        bt = b_ref[j] * pinv
        kt = k_ref[j] * pinv
        v = v_ref[j]
        for p in range(npairs):
            lhs.append(jnp.concatenate([stack(at, p), stack(rt, p)], axis=0).astype(BF16))
            rhs.append(jnp.concatenate([stack(bt, p), stack(kt, p)], axis=0))
            vm.append(stack(v, p).astype(BF16))
            p_last.append(pin[L - 1:L, p * LANES:(p + 1) * LANES])
    n = len(inst)
    gm = [_dot_nt(lhs[i], rhs[i]) for i in range(n)]
    a_ab = [jnp.where(tril_s, gm[i][:2 * L, :2 * L], 0.0) for i in range(n)]
    a_kr = [jnp.concatenate([jnp.where(tril_s, gm[i][:2 * L, 2 * L:], 0.0),
                             jnp.where(tril_i, gm[i][2 * L:, 2 * L:], 0.0)], axis=0) for i in range(n)]
    a_rb = [jnp.where(tril_i, gm[i][2 * L:, :2 * L], 0.0) for i in range(n)]
    rhs_t = [rhs[i].T.astype(BF16) for i in range(n)]
    decay = [jnp.broadcast_to(p_last[i], (LANES, LANES)).T for i in range(n)]
    lhs2 = [jnp.concatenate([lhs[i], a_kr[i].astype(BF16)], axis=1) for i in range(n)]
    t = [eye + a_ab[i] for i in range(n)]
    ak = [_dot(a_ab[i], a_ab[i]) for i in range(n)]
    for _ in range(int(math.log2(L)) - 2):
        pr = [_dot(ak[i], jnp.concatenate([ak[i], t[i]], axis=1)) for i in range(n)]
        ak = [pr[i][:, :2 * L] for i in range(n)]
        t = [t[i] + pr[i][:, 2 * L:] for i in range(n)]
    t = [t[i] + _dot(ak[i], t[i]) for i in range(n)]
    tc = [jnp.concatenate([t[i], _dot(a_rb[i], t[i])], axis=0) for i in range(n)]
    h = [g_scr[j, p] for (j, p) in inst]
    xy = [_dot(lhs2[i], jnp.concatenate([h[i].astype(BF16), vm[i]], axis=0)) for i in range(n)]
    uy = [_dot(tc[i], xy[i][:2 * L]) for i in range(n)]
    for i, (j, p) in enumerate(inst):
        yy = xy[i][2 * L:] + uy[i][2 * L:]
        y_ref[j, :, p * LANES:(p + 1) * LANES] = yy[:L] + yy[L:]
        uv = jnp.concatenate([uy[i][:2 * L].astype(BF16), vm[i]], axis=0)
        g_scr[j, p] = (h[i] + _dot(rhs_t[i], uv)) * decay[i]

    @pl.when(ci == pl.num_programs(1) - 1)
    def _():
        for j, p in inst:
            s_ref[j, p] = g_scr[j, p].T


def wkv_chunked(r, lw, k, v, a, b, nb=1):
    bsz, t, c = r.shape
    L = WKV_CHUNK
    assert t % L == 0 and c % LANES == 0 and bsz % nb == 0
    npairs = c // LANES
    seq = pl.BlockSpec((nb, L, c), lambda bi, ci: (bi, ci, 0))
    y, s = pl.pallas_call(
        functools.partial(_wkv_chunk_kernel, L=L, npairs=npairs, nb=nb),
        grid=(bsz // nb, t // L),
        in_specs=[seq] * 6,
        out_specs=[seq, pl.BlockSpec((nb, npairs, LANES, LANES), lambda bi, ci: (bi, 0, 0, 0))],
        out_shape=[jax.ShapeDtypeStruct((bsz, t, c), F32),
                   jax.ShapeDtypeStruct((bsz, npairs, LANES, LANES), F32)],
        scratch_shapes=[pltpu.VMEM((nb, npairs, LANES, LANES), F32)],
        compiler_params=_cparams("parallel", "arbitrary"),
        name="wkv_chunked",
    )(r, lw, k, v, a, b)
    n = HEAD_SIZE
    s = jnp.stack([s[:, :, :n, :n], s[:, :, n:, n:]], axis=2)
    return y, s.reshape(bsz, 2 * npairs, n, n)


def _wkv_step_kernel(*refs, chained):
    s_ref, r_ref, lw_ref, k_ref, v_ref, a_ref, b_ref, so_ref, y_ref = refs[1:] if chained else refs
    s = s_ref[...]
    sa = jnp.sum(s * a_ref[0][None], axis=1)
    s = s * jnp.exp(lw_ref[0])[None] + sa[:, None, :] * b_ref[0][None] + v_ref[0][:, None, :] * k_ref[0][None]
    so_ref[...] = s
    y_ref[0] = jnp.sum(s * r_ref[0][None], axis=1)


def _chain_out(prev, shape):
    if prev is None:
        return [], [], {}
    assert prev.shape == tuple(shape)
    return [pl.BlockSpec(memory_space=pl.ANY)], [prev], {0: 0}


def wkv_step(s_in, s_out, layer, r, lw, k, v, a, b):
    _, nh, n, _, bsz = s_in.shape
    vec = pl.BlockSpec((1, n, bsz), lambda h: (h, 0, 0))
    st = pl.BlockSpec((None, None, n, n, bsz), lambda h: (layer, h, 0, 0, 0))
    hv = lambda x: x.T.reshape(nh, n, bsz)
    c_specs, c_ops, c_alias = _chain_out(s_out, s_in.shape)
    s_out, y = pl.pallas_call(
        functools.partial(_wkv_step_kernel, chained=bool(c_ops)),
        grid=(nh,),
        in_specs=c_specs + [st] + [vec] * 6,
        out_specs=[st, vec],
        out_shape=[jax.ShapeDtypeStruct(s_in.shape, F32), jax.ShapeDtypeStruct((nh, n, bsz), F32)],
        input_output_aliases=c_alias,
        compiler_params=_cparams("parallel"),
        name="wkv_step",
    )(*c_ops, s_in, hv(r), hv(lw), hv(k), hv(v), hv(a), hv(b))
    return y.reshape(nh * n, bsz).T, s_out


CONV_PAD = 32


CONV_ROWS = 64


def _ln_swish(y, g_ref, b_ref):
    mu = jnp.mean(y, axis=-1, keepdims=True)
    dlt = y - mu
    var = jnp.mean(dlt * dlt, axis=-1, keepdims=True)
    y = dlt * lax.rsqrt(var + LN_EPS) * g_ref[...] + b_ref[...]
    return y * _sigmoid(y)


def _conv_seq_kernel(u_ref, buf_ref, cw_ref, cb_ref, lg_ref, lb_ref, o_ref, nb_ref, xp_scr, *, tt):
    ti = pl.program_id(1)

    @pl.when(ti == 0)
    def _():
        xp_scr[0:CONV_PAD, :] = buf_ref[0]

    xp_scr[CONV_PAD:CONV_PAD + tt, :] = u_ref[0]
    base = CONV_PAD - (CONV_WIDTH - 1)
    rb = CONV_ROWS

    for i in range(tt // rb):
        s0 = i * rb
        acc = None
        for r in range(SUBLANES):
            rows = rb if r == 0 else rb + SUBLANES
            part = None
            for j in range((CONV_PAD + SUBLANES) // SUBLANES):
                w = SUBLANES * j + r - base
                if 0 <= w < CONV_WIDTH:
                    term = xp_scr[pl.ds(s0 + SUBLANES * j, rows), :] * cw_ref[w:w + 1, :]
                    part = term if part is None else part + term
            part = part if r == 0 else part[r:r + rb]
            acc = part if acc is None else acc + part
        o_ref[0, pl.ds(s0, rb), :] = _ln_swish(acc + cb_ref[...], lg_ref, lb_ref)
    tail = xp_scr[tt:tt + CONV_PAD, :]
    xp_scr[0:CONV_PAD, :] = tail

    @pl.when(ti == pl.num_programs(1) - 1)
    def _():
        nb_ref[0] = tail


def conv_seq(u, buf0, lpk, tt):
    bsz, t, c = u.shape
    tt = _row_tile(t, tt)
    assert tt % CONV_ROWS == 0
    pad = CONV_PAD - (CONV_WIDTH - 1)
    bufp = jnp.pad(buf0, ((0, 0), (pad, 0), (0, 0)))
    params = [lpk[n] for n in ("conv_w", "conv_b", "conv_ln_g", "conv_ln_b")]
    hist = pl.BlockSpec((1, CONV_PAD, c), lambda bi, ti: (bi, 0, 0))
    out, nb = pl.pallas_call(
        functools.partial(_conv_seq_kernel, tt=tt),
        grid=(bsz, t // tt),
        in_specs=[pl.BlockSpec((1, tt, c), lambda bi, ti: (bi, ti, 0)), hist] + [_full(a.shape) for a in params],
        out_specs=[pl.BlockSpec((1, tt, c), lambda bi, ti: (bi, ti, 0)), hist],
        out_shape=[jax.ShapeDtypeStruct((bsz, t, c), F32), jax.ShapeDtypeStruct((bsz, CONV_PAD, c), F32)],
        scratch_shapes=[pltpu.VMEM((CONV_PAD + tt, c), F32)],
        compiler_params=_cparams("parallel", "arbitrary"),
        name="conv_seq",
    )(u, bufp, *params)
    return out, nb[:, pad:]


def _conv_step_kernel(*refs, chained):
    u_ref, buf_ref, cw_ref, cb_ref, lg_ref, lb_ref, nb_ref, o_ref = refs[1:] if chained else refs
    hist = CONV_WIDTH - 1
    glu = u_ref[...]
    acc = glu * cw_ref[hist:hist + 1, :] + cb_ref[...]
    for t in range(hist):
        slab = buf_ref[t]
        acc = acc + slab * cw_ref[t:t + 1, :]
        if t > 0:
            nb_ref[t - 1] = slab
    nb_ref[hist - 1] = glu
    o_ref[...] = _ln_swish(acc, lg_ref, lb_ref)


def conv_step(u, buf_in, buf_out, layer, lpk):
    bsz, c = u.shape
    params = [lpk[n] for n in ("conv_w", "conv_b", "conv_ln_g", "conv_ln_b")]
    hist = pl.BlockSpec((None, CONV_WIDTH - 1, bsz, c), lambda i: (layer, 0, 0, 0))
    c_specs, c_ops, c_alias = _chain_out(buf_out, buf_in.shape)
    buf_out, out = pl.pallas_call(
        functools.partial(_conv_step_kernel, chained=bool(c_ops)),
        grid=(1,),
        in_specs=c_specs + [_full((bsz, c)), hist] + [_full(a.shape) for a in params],
        out_specs=[hist, _full((bsz, c))],
        out_shape=[jax.ShapeDtypeStruct(buf_in.shape, F32), jax.ShapeDtypeStruct((bsz, c), F32)],
        input_output_aliases=c_alias,
        compiler_params=_cparams("parallel"),
        name="conv_step",
    )(*c_ops, u, buf_in, *params)
    return out, buf_out


def _merge_kernel(x_ref, y_ref, r_ref, k_ref, v_ref, g_ref, cv_ref, gate_ref, bd_ref, lg_ref, lb_ref, rk_ref,
                  wa_ref, wp_ref, bp_ref, wo_ref, gp_ref, o_ref, *, d):
    bd = bd_ref[...]
    inv_n = 1.0 / HEAD_SIZE
    y = y_ref[...]
    mu = _segsum(y, bd) * inv_n
    dlt = y - mu
    var = _segsum(dlt * dlt, bd) * inv_n
    yn = dlt * lax.rsqrt(var + GN_EPS) * lg_ref[...] + lb_ref[...]
    v = v_ref[...]
    bonus = _segsum(r_ref[...] * k_ref[...] * rk_ref[...], bd) * v
    za = (yn + bonus) * g_ref[...]
    ya = _dot(za, wa_ref[...])
    yb = _dot(cv_ref[...], wp_ref[...]) + bp_ref[...]
    mixed = _dot(gate_ref[:, :d] * ya + gate_ref[:, d:] * yb, wo_ref[...])
    o_ref[...] = x_ref[...] + _rms(mixed, gp_ref[...])


def merge(x, y, r, k, v, g, cv, gate, lpk, tm):
    m, d = x.shape
    c = y.shape[1]
    tm = _row_tile(m, tm)
    row = lambda n: pl.BlockSpec((tm, n), lambda i: (i, 0))
    names = ("bd", "lnx_g", "lnx_b", "r_k", "w_out_rwkv", "w_pw", "b_pw", "w_out", "g_mix_post")
    params = [lpk[n] for n in names]
    return pl.pallas_call(
        functools.partial(_merge_kernel, d=d),
        grid=(m // tm,),
        in_specs=[row(d)] + [row(c)] * 6 + [row(2 * d)] + [_full(a.shape) for a in params],
        out_specs=row(d),
        out_shape=jax.ShapeDtypeStruct((m, d), F32),
        compiler_params=_cparams("parallel"),
        name="merge",
    )(x, y, r, k, v, g, cv, gate, *params)


def _xattn_kernel(x_ref, mk_ref, mv_ref, gq_ref, wq_ref, wo_ref, gp_ref, o_ref, mk_scr, mv_scr, *, dh):
    @pl.when(pl.program_id(1) == 0)
    def _():
        mk_scr[...] = mk_ref[0].astype(BF16)
        mv_scr[...] = mv_ref[0].astype(BF16)

    x = x_ref[0]
    q = _dot(_rms(x, gq_ref[...]), wq_ref[...])
    scale = dh ** -0.5
    outs = []
    for h in range(N_XHEADS):
        sl = slice(h * dh, (h + 1) * dh)
        s = _dot_nt(q[:, sl], mk_scr[:, sl]) * scale
        s = s - jnp.max(s, axis=-1, keepdims=True)
        e = jnp.exp(s)
        pr = e / jnp.sum(e, axis=-1, keepdims=True)
        outs.append(_dot(pr, mv_scr[:, sl]))
    o = jnp.concatenate(outs, axis=1)
    o_ref[0] = x + _rms(_dot(o, wo_ref[...]), gp_ref[...])


def xattn(x, mk, mv, lpk, tq):
    bsz, t, d = x.shape
    nm = mk.shape[1]
    tq = _row_tile(t, tq)
    params = [lpk[n] for n in ("g_x_pre", "w_q", "w_o_x", "g_x_post")]
    xs = pl.BlockSpec((1, tq, d), lambda bi, ti: (bi, ti, 0))
    ms = pl.BlockSpec((1, nm, d), lambda bi, ti: (bi, 0, 0))
    return pl.pallas_call(
        functools.partial(_xattn_kernel, dh=d // N_XHEADS),
        grid=(bsz, t // tq),
        in_specs=[xs, ms, ms] + [_full(a.shape) for a in params],
        out_specs=xs,
        out_shape=jax.ShapeDtypeStruct(x.shape, F32),
        scratch_shapes=[pltpu.VMEM((nm, d), BF16), pltpu.VMEM((nm, d), BF16)],
        compiler_params=_cparams("parallel", "arbitrary"),
        name="xattn",
    )(x, mk, mv, *params)


def _xattn_step_kernel(q_ref, k_ref, v_ref, o_ref, *, bb, nm, nh, dh):
    assert 2 * nh == SUBLANES
    g = nm * nh // SUBLANES
    for j in range(bb):
        q = q_ref[j]
        q8 = jnp.concatenate([q, q], axis=0)
        kf = k_ref[j].reshape(nm * nh, dh).reshape(g, SUBLANES, dh)
        vf = v_ref[j].reshape(nm * nh, dh).reshape(g, SUBLANES, dh)
        s = jnp.sum(kf * q8[None], axis=-1, keepdims=True) * dh ** -0.5
        mx = jnp.max(s, axis=0, keepdims=True)
        mx = jnp.maximum(mx, pltpu.roll(mx[0], nh, axis=0)[None])
        e = jnp.exp(s - mx)
        den = jnp.sum(e, axis=0, keepdims=True)
        den = den + pltpu.roll(den[0], nh, axis=0)[None]
        o8 = jnp.sum((e / den) * vf, axis=0)
        o_ref[j] = o8[:nh] + o8[nh:]


def xattn_step(q, cache_k, cache_v, layer):
    bsz, d = q.shape
    _, _, nm, nh, dh = cache_k.shape
    bb = _row_tile(bsz, 2)
    qs = pl.BlockSpec((bb, nh, dh), lambda i: (i, 0, 0))
    cs = pl.BlockSpec((None, bb, nm, nh, dh), lambda i: (layer, i, 0, 0, 0))
    o = pl.pallas_call(
        functools.partial(_xattn_step_kernel, bb=bb, nm=nm, nh=nh, dh=dh),
        grid=(bsz // bb,),
        in_specs=[qs, cs, cs],
        out_specs=qs,
        out_shape=jax.ShapeDtypeStruct((bsz, nh, dh), F32),
        compiler_params=_cparams("parallel"),
        name="xattn_step",
    )(q.reshape(bsz, nh, dh), cache_k, cache_v)
    return o.reshape(bsz, d)


def _mm_norm_res_kernel(a_ref, x_ref, w_ref, g_ref, o_ref):
    o_ref[...] = x_ref[...] + _rms(_dot(a_ref[...], w_ref[...]), g_ref[...])


def mm_norm_res(a, x, w, g, tm):
    m, k = a.shape
    n = w.shape[1]
    tm = _row_tile(m, tm)
    return pl.pallas_call(
        _mm_norm_res_kernel,
        grid=(m // tm,),
        in_specs=[pl.BlockSpec((tm, k), lambda i: (i, 0)), pl.BlockSpec((tm, n), lambda i: (i, 0)),
                  _full((k, n)), _full((1, n))],
        out_specs=pl.BlockSpec((tm, n), lambda i: (i, 0)),
        out_shape=jax.ShapeDtypeStruct((m, n), F32),
        compiler_params=_cparams("parallel"),
        name="mm_norm_res",
    )(a, x, w, g)


MXU_K = 256


def _ffn_chunks(dff):
    half = -(-(dff // 2) // MXU_K) * MXU_K
    return ((0, half), (half, dff)) if half < dff else ((0, dff),)


def _ffn_kernel(x_ref, gpre_ref, wgu_ref, wd_ref, gpost_ref, o_ref, *, dff):
    x = x_ref[...]
    h = _rms(x, gpre_ref[...]).astype(BF16)
    acc = None
    for lo, hi in _ffn_chunks(dff):
        gt = jnp.dot(h, wgu_ref[:, lo:hi], preferred_element_type=F32)
        up = jnp.dot(h, wgu_ref[:, dff + lo:dff + hi], preferred_element_type=F32)
        part = _dot(gt * _sigmoid(gt) * up, wd_ref[lo:hi, :])
        acc = part if acc is None else acc + part
    o_ref[...] = x + _rms(acc, gpost_ref[...])


def _resident(shape):
    nd = len(shape)
    return pl.BlockSpec(shape, lambda *_: (0,) * nd, pipeline_mode=pl.Buffered(1))


def ffn(x, lpk, tm):
    m, d = x.shape
    w_gu, w_dn = lpk["w_gate_up"], lpk["w_down"]
    dff = w_dn.shape[0]
    assert dff % LANES == 0
    tm = _row_tile(m, tm)
    return pl.pallas_call(
        functools.partial(_ffn_kernel, dff=dff),
        grid=(m // tm,),
        in_specs=[pl.BlockSpec((tm, d), lambda i: (i, 0)), _full((1, d)), _resident(w_gu.shape),
                  _resident(w_dn.shape), _full((1, d))],
        out_specs=pl.BlockSpec((tm, d), lambda i: (i, 0)),
        out_shape=jax.ShapeDtypeStruct((m, d), F32),
        compiler_params=_cparams("parallel"),
        name="ffn",
    )(x, lpk["g_ffn_pre"], w_gu, w_dn, lpk["g_ffn_post"])


def _layer_params(l, w):
    row = lambda a: a[l].reshape(1, -1)
    c = w["w_out_rwkv"].shape[1]
    lora_rows = D_DECAY_LORA + D_ICLR_LORA
    w2p = jnp.zeros((lora_rows, c), F32).at[:D_DECAY_LORA].set(w["decay_w2"][l])
    a2p = jnp.zeros((lora_rows, c), F32).at[D_DECAY_LORA:].set(w["iclr_a2"][l])
    head = jnp.arange(c) // HEAD_SIZE
    lpk = {
        "c_rwkv": c,
        "bd": (head[:, None] == head[None, :]).astype(BF16),
        "decay_w2p": w2p.astype(BF16), "iclr_a2p": a2p.astype(BF16),
        "conv_w": jnp.pad(w["conv_w"][l], ((0, CONV_PAD - CONV_WIDTH), (0, 0))),
    }
    for n in ("w_in", "gate_g2", "w_out_rwkv", "w_pw", "w_out", "w_q", "w_o_x", "w_gate_up", "w_down"):
        lpk[n] = w[n][l].astype(BF16)
    lpk["w_kv"] = jnp.concatenate([w["w_k"][l], w["w_v"][l]], axis=1).astype(BF16)
    for n in ("tshift_mu", "decay_w0", "iclr_a0", "k_k", "k_a", "r_k", "lnx_g", "lnx_b", "glu_b", "conv_b",
              "conv_ln_g", "conv_ln_b", "b_pw", "merge_gate_b", "g_mix_pre", "g_mix_post", "g_mem", "g_x_pre",
              "g_x_post", "g_ffn_pre", "g_ffn_post"):
        lpk[n] = row(w[n])
    return lpk


PROMPT_TM = 256
SAMPLE_TM = 128


def _prompt_layer(x, mem_k, mem_v, shift0, buf0, lpk):
    bsz, t, d = x.shape
    m = bsz * t
    c = lpk["c_rwkv"]
    x2 = x.reshape(m, d)
    (r, lw, k, v, a, b, g, glu, gate), shift1 = proj_in(x2, shift0, t, lpk, PROMPT_TM)
    seq = lambda z: z.reshape(bsz, t, c)
    y, s1 = wkv_chunked(seq(r), seq(lw), seq(k), seq(v), seq(a), seq(b), nb=math.gcd(bsz, WKV_SEQS))
    cv, buf1 = conv_seq(seq(glu), buf0, lpk, 256)
    x2 = merge(x2, y.reshape(m, c), r, k, v, g, cv.reshape(m, c), gate, lpk, PROMPT_TM)
    x2 = xattn(x2.reshape(bsz, t, d), mem_k, mem_v, lpk, 512 if t % 512 == 0 else 256).reshape(m, d)
    x2 = ffn(x2, lpk, 2 * PROMPT_TM)
    return x2.reshape(bsz, t, d), s1, shift1, buf1


def _sample_layer(x2, cache_k, cache_v, s_in, s_out, shift0, buf_in, buf_out, layer, lpk):
    d = x2.shape[1]
    (r, lw, k, v, a, b, g, glu, gate), shift1 = proj_in(x2, shift0, 1, lpk, SAMPLE_TM)
    y, s_out = wkv_step(s_in, s_out, layer, r, lw, k, v, a, b)
    cv, buf_out = conv_step(glu, buf_in, buf_out, layer, lpk)
    x2 = merge(x2, y, r, k, v, g, cv, gate, lpk, SAMPLE_TM)
    (q,) = norm_mm(x2, lpk["g_x_pre"], lpk["w_q"], (d,), SAMPLE_TM)
    o = xattn_step(q, cache_k, cache_v, layer)
    x2 = mm_norm_res(o, x2, lpk["w_o_x"], lpk["g_x_post"], SAMPLE_TM)
    x2 = ffn(x2, lpk, SAMPLE_TM)
    return x2, s_out, shift1, buf_out


def kernel(x_prompt, x_sample, mem_prompt, state_rwkv, state_shift, state_conv, cache_mem_k, cache_mem_v, w_in, tshift_mu, decay_w0, decay_w2, iclr_a0, iclr_a2, gate_g2, k_k, k_a, r_k, lnx_g, lnx_b, w_out_rwkv, glu_b, conv_w, conv_b, conv_ln_g, conv_ln_b, w_pw, b_pw, merge_gate_b, w_out, g_mix_pre, g_mix_post, g_mem, w_q, w_k, w_v, w_o_x, g_x_pre, g_x_post, g_ffn_pre, g_ffn_post, w_gate_up, w_down):
    w = dict(w_in=w_in, tshift_mu=tshift_mu, decay_w0=decay_w0, decay_w2=decay_w2, iclr_a0=iclr_a0,
             iclr_a2=iclr_a2, gate_g2=gate_g2, k_k=k_k, k_a=k_a, r_k=r_k, lnx_g=lnx_g, lnx_b=lnx_b,
             w_out_rwkv=w_out_rwkv, glu_b=glu_b, conv_w=conv_w, conv_b=conv_b, conv_ln_g=conv_ln_g,
             conv_ln_b=conv_ln_b, w_pw=w_pw, b_pw=b_pw, merge_gate_b=merge_gate_b, w_out=w_out,
             g_mix_pre=g_mix_pre, g_mix_post=g_mix_post, g_mem=g_mem, w_q=w_q, w_k=w_k, w_v=w_v, w_o_x=w_o_x,
             g_x_pre=g_x_pre, g_x_post=g_x_post, g_ffn_pre=g_ffn_pre, g_ffn_post=g_ffn_post,
             w_gate_up=w_gate_up, w_down=w_down)
    depth = w_in.shape[0]
    bp, _, d = x_prompt.shape
    bs = x_sample.shape[0]
    nm = mem_prompt.shape[1]
    c = w_out_rwkv.shape[1]
    wr = state_shift.shape[-1]
    xp, xs = x_prompt, x_sample.reshape(bs, d)
    mem2 = mem_prompt.reshape(bp * nm, d)
    s_all, buf_all = None, None
    state_t = jnp.transpose(state_rwkv, (0, 2, 3, 4, 1))
    conv_t = jnp.transpose(state_conv, (0, 2, 1, 3))
    outs = {n: [] for n in ("rwkv_p", "shift_p", "conv_p", "memk_p", "memv_p", "shift_s")}
    dh = d // N_XHEADS
    for l in range(depth):
        lpk = _layer_params(l, w)
        mk, mv = norm_mm(mem2, lpk["g_mem"], lpk["w_kv"], (d, d), PROMPT_TM)
        mk, mv = mk.reshape(bp, nm, d), mv.reshape(bp, nm, d)
        xp, s_p, sh_p, cb_p = _prompt_layer(xp, mk, mv, jnp.zeros((bp, wr), F32),
                                            jnp.zeros((bp, CONV_WIDTH - 1, c), F32), lpk)
        xs, s_all, sh_s, buf_all = _sample_layer(xs, cache_mem_k, cache_mem_v, state_t, s_all, state_shift[l],
                                                 conv_t, buf_all, l, lpk)
        outs["rwkv_p"].append(s_p); outs["shift_p"].append(sh_p); outs["conv_p"].append(cb_p)
        outs["memk_p"].append(mk.reshape(bp, nm, N_XHEADS, dh)); outs["memv_p"].append(mv.reshape(bp, nm, N_XHEADS, dh))
        outs["shift_s"].append(sh_s)
    st = {n: jnp.stack(v) for n, v in outs.items()}
    return (xp, xs.reshape(bs, 1, d), st["rwkv_p"], st["shift_p"], st["conv_p"], st["memk_p"], st["memv_p"],
            jnp.transpose(s_all, (0, 4, 1, 2, 3)), st["shift_s"], jnp.transpose(buf_all, (0, 2, 1, 3)))
```
